```python
import math
import jax, jax.numpy as jnp
from jax import lax
import numpy as np

D_MODEL = 2048
BATCH = 2
SEQ = 8192
DEPTH = 4
DEC_BATCH = 4
DEC_SEQ = 8192
PAST_LEN = 128

MIX_WIDTH = D_MODEL
HYENA_WIDTH = MIX_WIDTH // 2
ATTN_WIDTH = MIX_WIDTH - HYENA_WIDTH
N_HEADS = 8
HEAD_DIM = ATTN_WIDTH // (2 * N_HEADS)
V_HEAD_DIM = 2 * HEAD_DIM
HYENA_ORDER = 2
HYENA_IN = (HYENA_ORDER + 1) * HYENA_WIDTH
IN_WIDTH = HYENA_IN + 3 * ATTN_WIDTH
SHORT_CONV = 3
FILTER_EMB = 33
FILTER_BANDS = (FILTER_EMB - 1) // 2
FILTER_HIDDEN = 64
DECAY_TARGET = 1e-2
FAST_DECAY_PCT = 0.3
SLOW_DECAY_PCT = 1.5
MIN_DECAY = math.log(DECAY_TARGET) / FAST_DECAY_PCT
MAX_DECAY = math.log(DECAY_TARGET) / SLOW_DECAY_PCT
D_FF = (8 * D_MODEL + 3 * 256 - 1) // (3 * 256) * 256
ROPE_THETA = 10000.0
Q_BLOCK = 128
EPS = 1e-6

kernel_name = 'hymba_hyena_diffattn_encoder'


def rms_norm(x, g):
    xf = x.astype(jnp.float32)
    y = xf * lax.rsqrt(jnp.mean(xf * xf, axis=-1, keepdims=True) + EPS)
    return (y * g.astype(jnp.float32)).astype(x.dtype)


def rope_tables(L):
    inv_freq = 1.0 / (ROPE_THETA ** (jnp.arange(0, HEAD_DIM, 2, dtype=jnp.float32) / HEAD_DIM))
    ang = jnp.arange(L, dtype=jnp.float32)[:, None] * inv_freq[None, :]
    return jnp.cos(ang), jnp.sin(ang)


def apply_rope(x, cos, sin):
    x1, x2 = jnp.split(x, 2, axis=-1)
    c = cos[None, :, None, :]
    s = sin[None, :, None, :]
    return jnp.concatenate([x1 * c - x2 * s, x1 * s + x2 * c], axis=-1)


def short_conv(u, w, b):
    L = u.shape[1]
    pad = SHORT_CONV // 2
    up = jnp.pad(u, ((0, 0), (pad, pad), (0, 0)))
    out = b
    for j in range(SHORT_CONV):
        out = out + up[:, j:j + L] * w[j]
    return out


def hyena_filters(L, w1, b1, f1, w2, b2, f2, w3, b3, f3, w4):
    f32 = jnp.float32
    t = jnp.linspace(0.0, 1.0, L, dtype=f32)[:, None]
    omega = 2.0 * math.pi * jnp.arange(L, dtype=f32) / L
    bands = jnp.linspace(1e-4, FILTER_BANDS - 1, FILTER_BANDS, dtype=f32)
    phase = omega[:, None] * bands[None, :]
    z = jnp.concatenate([t, jnp.cos(phase), -jnp.sin(phase)], axis=-1)
    h = jnp.sin(f1.astype(f32) * (z @ w1.astype(f32) + b1.astype(f32)))
    h = jnp.sin(f2.astype(f32) * (h @ w2.astype(f32) + b2.astype(f32)))
    h = jnp.sin(f3.astype(f32) * (h @ w3.astype(f32) + b3.astype(f32)))
    h = (h @ w4.astype(f32)).reshape(L, HYENA_ORDER, 2, HYENA_WIDTH)
    deltas = jnp.abs(jnp.linspace(MIN_DECAY, MAX_DECAY, HYENA_WIDTH, dtype=f32))
    h = h * jnp.exp(-t * deltas[None, :])[:, None, None, :]
    fwd = h[:, :, 0]
    bwd = h[:, :, 1]
    k = jnp.concatenate([fwd[:1] + bwd[:1], fwd[1:], jnp.zeros_like(fwd[:1]), bwd[:0:-1]], axis=0)
    return jnp.fft.rfft(k, axis=0)


def long_conv(u, k_f, bias):
    L = u.shape[1]
    u_f = jnp.fft.rfft(u, n=2 * L, axis=1)
    y = jnp.fft.irfft(u_f * k_f[None], n=2 * L, axis=1)[:, :L]
    return y + u * bias


def diff_attention(q, k, v, lam, lam_init, subln):
    B, L = q.shape[0], q.shape[1]
    n_blocks = L // Q_BLOCK
    scale = HEAD_DIM ** -0.5
    q_blocks = q.reshape(B, n_blocks, Q_BLOCK, 2 * N_HEADS, HEAD_DIM).transpose(1, 0, 2, 3, 4)

    def one_block(qb):
        s = jnp.einsum('bqhd,bkhd->bhqk', qb, k) * scale
        p = jax.nn.softmax(s, axis=-1).reshape(B, N_HEADS, 2, Q_BLOCK, L)
        a = p[:, :, 0] - lam * p[:, :, 1]
        return jnp.einsum('bhqk,bkhe->bqhe', a, v)

    o = lax.map(one_block, q_blocks)
    o = o.transpose(1, 0, 2, 3, 4).reshape(B, L, N_HEADS, V_HEAD_DIM)
    o = rms_norm(o, subln) * (1.0 - lam_init)
    return o.reshape(B, L, ATTN_WIDTH)


def encoder_layer(x, cos, sin, lam_init, g_mix_pre, g_mix_post, g_ffn_pre, g_ffn_post,
                  w_in, conv_w, conv_b, filt_w1, filt_b1, filt_f1, filt_w2, filt_b2, filt_f2,
                  filt_w3, filt_b3, filt_f3, filt_w4, hyena_bias, hyena_norm,
                  lam_q1, lam_k1, lam_q2, lam_k2, subln, w_out, w_gate, w_up, w_down):
    f32 = jnp.float32
    B, L, _ = x.shape
    h = rms_norm(x, g_mix_pre)
    proj = h @ w_in
    hy, q, k, v = jnp.split(proj, [HYENA_IN, HYENA_IN + ATTN_WIDTH, HYENA_IN + 2 * ATTN_WIDTH], axis=-1)

    u = short_conv(hy.astype(f32), conv_w.astype(f32), conv_b.astype(f32))
    x1, x2, hv = jnp.split(u, 3, axis=-1)
    k_f = hyena_filters(L, filt_w1, filt_b1, filt_f1, filt_w2, filt_b2, filt_f2,
                        filt_w3, filt_b3, filt_f3, filt_w4)
    z = x1 * long_conv(hv, k_f[:, 0], hyena_bias[0].astype(f32))
    z = x2 * long_conv(z, k_f[:, 1], hyena_bias[1].astype(f32))
    hy_out = rms_norm(z, hyena_norm).astype(x.dtype)

    q = apply_rope(q.astype(f32).reshape(B, L, 2 * N_HEADS, HEAD_DIM), cos, sin)
    k = apply_rope(k.astype(f32).reshape(B, L, 2 * N_HEADS, HEAD_DIM), cos, sin)
    v = v.astype(f32).reshape(B, L, N_HEADS, V_HEAD_DIM)
    lam = (jnp.exp(jnp.sum(lam_q1.astype(f32) * lam_k1.astype(f32)))
           - jnp.exp(jnp.sum(lam_q2.astype(f32) * lam_k2.astype(f32))) + lam_init)
    at_out = diff_attention(q, k, v, lam, lam_init, subln).astype(x.dtype)

    mix = jnp.concatenate([hy_out, at_out], axis=-1) @ w_out
    x = x + rms_norm(mix, g_mix_post)

    h = rms_norm(x, g_ffn_pre)
    ff = (jax.nn.silu(h @ w_gate) * (h @ w_up)) @ w_down
    return x + rms_norm(ff, g_ffn_post)


def run_trunk(x, weights):
    (norm_mix_pre, norm_mix_post, norm_ffn_pre, norm_ffn_post, w_in, conv_w, conv_b,
     filt_w1, filt_b1, filt_f1, filt_w2, filt_b2, filt_f2, filt_w3, filt_b3, filt_f3, filt_w4,
     hyena_bias, hyena_norm, lam_q1, lam_k1, lam_q2, lam_k2, subln,
     w_out, w_gate, w_up, w_down) = weights
    L = x.shape[1]
    cos, sin = rope_tables(L)
    for li in range(DEPTH):
        lam_init = 0.8 - 0.6 * math.exp(-0.3 * li)
        x = encoder_layer(x, cos, sin, lam_init,
                          norm_mix_pre[li], norm_mix_post[li], norm_ffn_pre[li], norm_ffn_post[li],
                          w_in[li], conv_w[li], conv_b[li],
                          filt_w1[li], filt_b1[li], filt_f1[li], filt_w2[li], filt_b2[li], filt_f2[li],
                          filt_w3[li], filt_b3[li], filt_f3[li], filt_w4[li],
                          hyena_bias[li], hyena_norm[li],
                          lam_q1[li], lam_k1[li], lam_q2[li], lam_k2[li], subln[li],
                          w_out[li], w_gate[li], w_up[li], w_down[li])
    return x


def setup_inputs(seed: int = 0) -> dict:
    key = jax.random.key(seed)
    ks = jax.random.split(key, 32)
    f32 = jnp.float32

    def nrm(k, shape, scale):
        return jax.random.normal(k, shape, f32) * scale

    def gain(k, n):
        return 1.0 + 0.02 * jax.random.normal(k, (DEPTH, n), f32)

    return {
        'x_prompt': nrm(ks[0], (BATCH, SEQ, D_MODEL), 1.0),
        'x_sample': nrm(ks[1], (DEC_BATCH, DEC_SEQ, D_MODEL), 1.0),
        'norm_mix_pre': gain(ks[2], D_MODEL),
        'norm_mix_post': gain(ks[3], D_MODEL),
        'norm_ffn_pre': gain(ks[4], D_MODEL),
        'norm_ffn_post': gain(ks[5], D_MODEL),
        'w_in': nrm(ks[6], (DEPTH, D_MODEL, IN_WIDTH), D_MODEL ** -0.5),
        'conv_w': nrm(ks[7], (DEPTH, SHORT_CONV, HYENA_IN), SHORT_CONV ** -0.5),
        'conv_b': nrm(ks[8], (DEPTH, HYENA_IN), 0.02),
        'filt_w1': nrm(ks[9], (DEPTH, FILTER_EMB, FILTER_HIDDEN), FILTER_EMB ** -0.5),
        'filt_b1': nrm(ks[10], (DEPTH, FILTER_HIDDEN), 0.02),
        'filt_f1': 1.0 + 0.1 * jax.random.normal(ks[11], (DEPTH, FILTER_HIDDEN), f32),
        'filt_w2': nrm(ks[12], (DEPTH, FILTER_HIDDEN, FILTER_HIDDEN), FILTER_HIDDEN ** -0.5),
        'filt_b2': nrm(ks[13], (DEPTH, FILTER_HIDDEN), 0.02),
        'filt_f2': 1.0 + 0.1 * jax.random.normal(ks[14], (DEPTH, FILTER_HIDDEN), f32),
        'filt_w3': nrm(ks[15], (DEPTH, FILTER_HIDDEN, FILTER_HIDDEN), FILTER_HIDDEN ** -0.5),
        'filt_b3': nrm(ks[16], (DEPTH, FILTER_HIDDEN), 0.02),
        'filt_f3': 1.0 + 0.1 * jax.random.normal(ks[17], (DEPTH, FILTER_HIDDEN), f32),
        'filt_w4': nrm(ks[18], (DEPTH, FILTER_HIDDEN, HYENA_ORDER * 2 * HYENA_WIDTH), FILTER_HIDDEN ** -0.5),
        'hyena_bias': nrm(ks[19], (DEPTH, HYENA_ORDER, HYENA_WIDTH), 1.0),
        'hyena_norm': gain(ks[20], HYENA_WIDTH),
        'lam_q1': nrm(ks[21], (DEPTH, HEAD_DIM), 0.1),
        'lam_k1': nrm(ks[22], (DEPTH, HEAD_DIM), 0.1),
        'lam_q2': nrm(ks[23], (DEPTH, HEAD_DIM), 0.1),
        'lam_k2': nrm(ks[24], (DEPTH, HEAD_DIM), 0.1),
        'subln': gain(ks[25], V_HEAD_DIM),
        'w_out': nrm(ks[26], (DEPTH, MIX_WIDTH, D_MODEL), MIX_WIDTH ** -0.5),
        'w_gate': nrm(ks[27], (DEPTH, D_MODEL, D_FF), D_MODEL ** -0.5),
        'w_up': nrm(ks[28], (DEPTH, D_MODEL, D_FF), D_MODEL ** -0.5),
        'w_down': nrm(ks[29], (DEPTH, D_FF, D_MODEL), D_FF ** -0.5),
    }


def reference(x_prompt, x_sample, norm_mix_pre, norm_mix_post, norm_ffn_pre, norm_ffn_post,
              w_in, conv_w, conv_b, filt_w1, filt_b1, filt_f1, filt_w2, filt_b2, filt_f2,
              filt_w3, filt_b3, filt_f3, filt_w4, hyena_bias, hyena_norm,
              lam_q1, lam_k1, lam_q2, lam_k2, subln, w_out, w_gate, w_up, w_down):
    weights = (norm_mix_pre, norm_mix_post, norm_ffn_pre, norm_ffn_post, w_in, conv_w, conv_b,
               filt_w1, filt_b1, filt_f1, filt_w2, filt_b2, filt_f2, filt_w3, filt_b3, filt_f3, filt_w4,
               hyena_bias, hyena_norm, lam_q1, lam_k1, lam_q2, lam_k2, subln,
               w_out, w_gate, w_up, w_down)
    y_prompt = run_trunk(x_prompt, weights)
    y_sample = run_trunk(x_sample, weights)
    return (y_prompt, y_sample)
```

```python
import functools
import math

import jax
import jax.numpy as jnp
from jax import lax
from jax.experimental import pallas as pl
from jax.experimental.pallas import tpu as pltpu

F32 = jnp.float32
BF16 = jnp.bfloat16

D_MODEL = 2048
DEPTH = 4
HYENA_WIDTH = 1024
ATTN_WIDTH = 1024
N_HEADS = 8
HEAD_DIM = 64
V_HEAD_DIM = 128
HYENA_IN = 3 * HYENA_WIDTH
IN_WIDTH = HYENA_IN + 3 * ATTN_WIDTH
FILTER_EMB = 33
FILTER_BANDS = 16
FILTER_HIDDEN = 64
DECAY_TARGET = 1e-2
MIN_DECAY = math.log(DECAY_TARGET) / 0.3
MAX_DECAY = math.log(DECAY_TARGET) / 1.5
D_FF = 5632
ROPE_THETA = 10000.0
EPS = 1e-6

LANES = 128
SUBLANES = 8
DFT_RADIX = 128
VMEM_LIMIT = 56 * 1024 * 1024


def _cparams(sem):
    return pltpu.CompilerParams(dimension_semantics=sem, vmem_limit_bytes=VMEM_LIMIT)


def _rms(xf, g):
    return xf * lax.rsqrt(jnp.mean(xf * xf, axis=-1, keepdims=True) + EPS) * g


IN_TN = 1024
HY_TILES = HYENA_IN // IN_TN


def _in_proj_kernel(x_ref, g_ref, w_ref, cos_ref, sa_ref, sb_ref,
                    hy_ref, q_ref, k_ref, v_ref, h_scr):
    j = pl.program_id(1)

    @pl.when(j == 0)
    def _():
        h_scr[...] = _rms(x_ref[...], g_ref[...]).astype(BF16)

    res = jnp.dot(h_scr[...], w_ref[...], preferred_element_type=F32)

    @pl.when(j < HY_TILES)
    def _():
        hy_ref[...] = res

    def rope(sl):
        return (sl * cos_ref[...] + pltpu.roll(sl, LANES - HEAD_DIM // 2, 1) * sa_ref[...]
                + pltpu.roll(sl, HEAD_DIM // 2, 1) * sb_ref[...])

    @pl.when(j == HY_TILES)
    def _():
        for h in range(N_HEADS):
            sl = res[:, h * LANES:(h + 1) * LANES]
            q_ref[h] = (rope(sl) * (HEAD_DIM ** -0.5)).astype(BF16)

    @pl.when(j == HY_TILES + 1)
    def _():
        for h in range(N_HEADS):
            k_ref[h] = rope(res[:, h * LANES:(h + 1) * LANES]).astype(BF16)

    @pl.when(j == HY_TILES + 2)
    def _():
        for h in range(N_HEADS):
            v_ref[h] = res[:, h * LANES:(h + 1) * LANES].astype(BF16)


def _in_proj(x, g, w, cos_t, sa_t, sb_t, n_seq, seq_len, tm=512):
    t_tok = x.shape[0]
    lt = seq_len // tm
    head_spec = pl.BlockSpec((None, N_HEADS, tm, LANES), lambda i, j: (i // lt, 0, i % lt, 0))
    tab_spec = pl.BlockSpec((tm, LANES), lambda i, j: (i % lt, 0))
    head_shape = jax.ShapeDtypeStruct((n_seq, N_HEADS, seq_len, LANES), BF16)
    return pl.pallas_call(
        _in_proj_kernel,
        grid=(t_tok // tm, IN_WIDTH // IN_TN),
        in_specs=[
            pl.BlockSpec((tm, D_MODEL), lambda i, j: (i, 0)),
            pl.BlockSpec((1, D_MODEL), lambda i, j: (0, 0)),
            pl.BlockSpec((D_MODEL, IN_TN), lambda i, j: (0, j)),
            tab_spec, tab_spec, tab_spec,
        ],
        out_specs=[
            pl.BlockSpec((tm, IN_TN), lambda i, j: (i, jnp.minimum(j, HY_TILES - 1))),
            head_spec, head_spec, head_spec,
        ],
        out_shape=[jax.ShapeDtypeStruct((t_tok, HYENA_IN), F32), head_shape, head_shape, head_shape],
        scratch_shapes=[pltpu.VMEM((tm, D_MODEL), BF16)],
        compiler_params=_cparams(("parallel", "arbitrary")),
        name="in_proj",
    )(x, g, w, cos_t, sa_t, sb_t)


def _short_conv_kernel(x_ref, w_ref, b_ref, o_ref):
    x = x_ref[...]
    n = x.shape[0]
    row = lax.broadcasted_iota(jnp.int32, x.shape, 0)
    prev = jnp.where(row == 0, 0.0, pltpu.roll(x, 1, 0))
    nxt = jnp.where(row == n - 1, 0.0, pltpu.roll(x, n - 1, 0))
    o_ref[...] = b_ref[...] + prev * w_ref[0:1, :] + x * w_ref[1:2, :] + nxt * w_ref[2:3, :]


def _short_conv(hy, w, b):
    n_seq, seq_len, width = hy.shape
    return pl.pallas_call(
        _short_conv_kernel,
        grid=(n_seq, width // LANES),
        in_specs=[
            pl.BlockSpec((None, seq_len, LANES), lambda s, c: (s, 0, c)),
            pl.BlockSpec((SUBLANES, LANES), lambda s, c: (0, c)),
            pl.BlockSpec((1, LANES), lambda s, c: (0, c)),
        ],
        out_specs=pl.BlockSpec((None, seq_len, LANES), lambda s, c: (s, 0, c)),
        out_shape=jax.ShapeDtypeStruct(hy.shape, F32),
        compiler_params=_cparams(("parallel", "parallel")),
        name="short_conv",
    )(hy, w, b)


def _filter_kernel(z_ref, w1_ref, b1_ref, f1_ref, w2_ref, b2_ref, f2_ref,
                   w3_ref, b3_ref, f3_ref, w4_ref, delta_ref, o_ref, *, seq_len):
    hp = lax.Precision.HIGHEST
    z = z_ref[...]
    h = jnp.sin(f1_ref[...] * (jnp.dot(z, w1_ref[...], precision=hp, preferred_element_type=F32) + b1_ref[...]))
    h = jnp.sin(f2_ref[...] * (jnp.dot(h, w2_ref[...], precision=hp, preferred_element_type=F32) + b2_ref[...]))
    h = jnp.sin(f3_ref[...] * (jnp.dot(h, w3_ref[...], precision=hp, preferred_element_type=F32) + b3_ref[...]))
    rb = z.shape[0]
    n = pl.program_id(0) * rb + lax.broadcasted_iota(jnp.int32, (rb, 1), 0)
    m_fwd = (n < seq_len).astype(F32)
    m_bwd = jnp.logical_or(n == 0, n > seq_len).astype(F32)
    decay = jnp.exp(-z[:, 0:1] * delta_ref[...])
    cw = HYENA_WIDTH
    for order in range(2):
        base = order * 2 * cw
        fwd = jnp.dot(h, w4_ref[:, base:base + cw], precision=hp, preferred_element_type=F32)
        bwd = jnp.dot(h, w4_ref[:, base + cw:base + 2 * cw], precision=hp, preferred_element_type=F32)
        o_ref[order] = (m_fwd * fwd + m_bwd * bwd) * decay


def _filters(zfeat, w1, b1, f1, w2, b2, f2, w3, b3, f3, w4, delta, seq_len, rb=512):
    n2 = zfeat.shape[0]
    full = lambda a: pl.BlockSpec(a.shape, lambda i: (0,) * a.ndim)
    args = (w1, b1, f1, w2, b2, f2, w3, b3, f3, w4, delta)
    return pl.pallas_call(
        functools.partial(_filter_kernel, seq_len=seq_len),
        grid=(n2 // rb,),
        in_specs=[pl.BlockSpec((rb, zfeat.shape[1]), lambda i: (i, 0))] + [full(a) for a in args],
        out_specs=pl.BlockSpec((2, rb, HYENA_WIDTH), lambda i: (0, i, 0)),
        out_shape=jax.ShapeDtypeStruct((2, n2, HYENA_WIDTH), F32),
        compiler_params=_cparams(("parallel",)),
        name="hyena_filter",
    )(zfeat, *args)


DFT_CB = 256
DFT_KK = 8


def _dft_a_kernel(u_ref, m_ref, o_ref):
    cb = u_ref.shape[-1]
    x = u_ref[...].reshape(DFT_RADIX * SUBLANES, cb).astype(BF16)
    r = jnp.dot(m_ref[...], x, preferred_element_type=F32)
    o_ref[...] = r.reshape(o_ref.shape)


def _dft_a(u6, m_a, ch_off):
    g = u6.shape[0]
    lead = u6.shape[1:-3]
    nbc = u6.shape[-3]
    cout = HYENA_WIDTH
    blk_in = (None,) + lead + (None, SUBLANES, DFT_CB)
    zeros = (0,) * len(lead)
    return pl.pallas_call(
        _dft_a_kernel,
        grid=(nbc, g, cout // DFT_CB),
        in_specs=[
            pl.BlockSpec(blk_in, lambda bc, p, c: (p,) + zeros + (bc, 0, c + ch_off // DFT_CB)),
            pl.BlockSpec((None,) + m_a.shape[1:], lambda bc, p, c: (bc, 0, 0)),
        ],
        out_specs=pl.BlockSpec((None, 2, DFT_RADIX, None, SUBLANES, DFT_CB),
                               lambda bc, p, c: (p, 0, 0, bc, 0, c)),
        out_shape=jax.ShapeDtypeStruct((g, 2, DFT_RADIX, nbc, SUBLANES, cout), F32),
        compiler_params=_cparams(("arbitrary", "arbitrary", "arbitrary")),
        name="dft_stage_a",
    )(u6, m_a)


def _dft_b_conv_kernel(a_ref, kf_ref, fb_ref, fbi_ref, g_ref):
    r = DFT_RADIX
    for i in range(DFT_KK):
        s = jnp.concatenate([a_ref[0, i], a_ref[1, i]], axis=0).astype(BF16)
        x = jnp.dot(fb_ref[...], s, preferred_element_type=F32)
        xr, xi = x[:r], x[r:]
        kr, ki = kf_ref[0, i], kf_ref[1, i]
        y = jnp.concatenate([xr * kr - xi * ki, xr * ki + xi * kr], axis=0).astype(BF16)
        gg = jnp.dot(fbi_ref[...], y, preferred_element_type=F32)
        g_ref[0, i] = gg[:r]
        g_ref[1, i] = gg[r:]


def _dft_b_conv(a1, kf, fb, fbi):
    p, _, r, _, c = a1.shape
    blk = pl.BlockSpec((None, 2, DFT_KK, r, DFT_CB), lambda k, cc, pp: (pp, 0, k, 0, cc))
    mat = pl.BlockSpec((2 * r, 2 * r), lambda k, cc, pp: (0, 0))
    return pl.pallas_call(
        _dft_b_conv_kernel,
        grid=(r // DFT_KK, c // DFT_CB, p),
        in_specs=[blk, pl.BlockSpec((2, DFT_KK, r, DFT_CB), lambda k, cc, pp: (0, k, 0, cc)), mat, mat],
        out_specs=blk,
        out_shape=jax.ShapeDtypeStruct(a1.shape, F32),
        compiler_params=_cparams(("parallel", "parallel", "arbitrary")),
        name="dft_stage_b_conv",
    )(a1, kf, fb, fbi)


def _dft_b_fwd_kernel(a_ref, fb_ref, o_ref, *, scale):
    r = DFT_RADIX
    for i in range(DFT_KK):
        s = jnp.concatenate([a_ref[0, i], a_ref[1, i]], axis=0).astype(BF16)
        x = jnp.dot(fb_ref[...], s, preferred_element_type=F32) * scale
        o_ref[0, i] = x[:r]
        o_ref[1, i] = x[r:]


def _dft_b_fwd(a1, fb, scale):
    p, _, r, _, c = a1.shape
    blk = pl.BlockSpec((None, 2, DFT_KK, r, DFT_CB), lambda k, cc, pp: (pp, 0, k, 0, cc))
    return pl.pallas_call(
        functools.partial(_dft_b_fwd_kernel, scale=scale),
        grid=(r // DFT_KK, c // DFT_CB, p),
        in_specs=[blk, pl.BlockSpec((2 * r, 2 * r), lambda k, cc, pp: (0, 0))],
        out_specs=blk,
        out_shape=jax.ShapeDtypeStruct(a1.shape, F32),
        compiler_params=_cparams(("parallel", "parallel", "arbitrary")),
        name="dft_stage_b_filter",
    )(a1, fb)


def _dft_c_gate_kernel(g_ref, m_ref, x_ref, v_ref, bias_ref, o_ref):
    cb = g_ref.shape[-1]
    g = g_ref[...].reshape(2 * DFT_RADIX * SUBLANES, cb).astype(BF16)
    y = jnp.dot(m_ref[...], g, preferred_element_type=F32).reshape(o_ref.shape)
    o_ref[...] = x_ref[...] * (y + v_ref[...] * bias_ref[...])


def _dft_c_gate(g6, m_c, xg, xg_off, v, v_off, bias):
    p, _, r, nbc, _, c = g6.shape
    half = r // 2
    tspec = lambda off: pl.BlockSpec((None, 2, half, None, SUBLANES, DFT_CB),
                                     lambda bc, pp, cc: (pp, 0, 0, bc, 0, cc + off // DFT_CB))
    return pl.pallas_call(
        _dft_c_gate_kernel,
        grid=(nbc, p, c // DFT_CB),
        in_specs=[
            pl.BlockSpec((None, 2, r, None, SUBLANES, DFT_CB), lambda bc, pp, cc: (pp, 0, 0, bc, 0, cc)),
            pl.BlockSpec((None,) + m_c.shape[1:], lambda bc, pp, cc: (bc, 0, 0)),
            tspec(xg_off), tspec(v_off),
            pl.BlockSpec((1, DFT_CB), lambda bc, pp, cc: (0, cc)),
        ],
        out_specs=tspec(0),
        out_shape=jax.ShapeDtypeStruct((p, 2, half, nbc, SUBLANES, c), F32),
        compiler_params=_cparams(("arbitrary", "arbitrary", "arbitrary")),
        name="dft_stage_c_gate",
    )(g6, m_c, xg, v, bias)


def _dft_tables(seq_len):
    r = DFT_RADIX
    n = 2 * seq_len
    nbc = r // SUBLANES
    i32 = jnp.int32
    bc = jnp.arange(nbc, dtype=i32).reshape(nbc, 1, 1, 1)
    k1 = jnp.arange(r, dtype=i32).reshape(1, r, 1, 1)
    jj = jnp.arange(SUBLANES, dtype=i32).reshape(1, 1, SUBLANES, 1)
    aa = jnp.arange(r, dtype=i32).reshape(1, 1, 1, r)
    ang = ((k1 * (r * aa + SUBLANES * bc + jj)) % n).astype(F32) * (2.0 * math.pi / n)
    c, s = jnp.cos(ang), jnp.sin(ang)
    eye = jnp.eye(SUBLANES, dtype=F32)

    def expand(t):
        return t[..., None] * eye[None, None, :, None, :]

    half = r // 2
    ch, sh = c[..., :half], s[..., :half]
    re_rows = jnp.stack([expand(ch), expand(sh)], axis=3)
    im_rows = jnp.stack([expand(-sh), expand(ch)], axis=3)
    m_pair = jnp.stack([re_rows, im_rows], axis=1).reshape(nbc, 2 * r * SUBLANES, r * SUBLANES)
    m_real = jnp.stack([expand(c), expand(-s)], axis=1).reshape(nbc, 2 * r * SUBLANES, r * SUBLANES)
    m_pair = m_pair.astype(BF16)
    m_inv = jnp.swapaxes(m_pair, 1, 2)
    kb = jnp.arange(r, dtype=i32)
    angb = ((kb[:, None] * kb[None, :]) % r).astype(F32) * (2.0 * math.pi / r)
    cb_, sb_ = jnp.cos(angb), jnp.sin(angb)
    fb = jnp.block([[cb_, sb_], [-sb_, cb_]]).astype(BF16)
    fbi = jnp.block([[cb_, -sb_], [sb_, cb_]]).astype(BF16)
    return m_pair, m_real.astype(BF16), m_inv, fb, fbi


def _filter_features(seq_len):
    n2 = 2 * seq_len
    n = jnp.arange(n2)
    pos = jnp.where(n <= seq_len, n, n2 - n)
    pos = jnp.minimum(pos, seq_len - 1)
    t_lin = jnp.linspace(0.0, 1.0, seq_len, dtype=F32)
    omega = 2.0 * math.pi * jnp.arange(seq_len, dtype=F32) / seq_len
    bands = jnp.linspace(1e-4, FILTER_BANDS - 1, FILTER_BANDS, dtype=F32)
    phase = omega[:, None] * bands[None, :]
    z = jnp.concatenate([t_lin[:, None], jnp.cos(phase), -jnp.sin(phase)], axis=-1)
    z = jnp.pad(z, ((0, 0), (0, FILTER_HIDDEN - FILTER_EMB)))
    return z[pos]


def _attn_kernel(lq1_ref, lk1_ref, lq2_ref, lk2_ref, subln_ref, q_ref, k_ref, v_ref, o_ref,
                 qm_scr, m_scr, l_scr, acc_scr, *, lam_init):
    kb = pl.program_id(3)

    @pl.when(kb == 0)
    def _():
        q = q_ref[...]
        lane = lax.broadcasted_iota(jnp.int32, q.shape, 1)
        zero = jnp.zeros_like(q)
        qm_scr[0] = jnp.where(lane < HEAD_DIM, q, zero)
        qm_scr[1] = jnp.where(lane >= HEAD_DIM, q, zero)
        m_scr[...] = jnp.full(m_scr.shape, -jnp.inf, F32)
        l_scr[...] = jnp.zeros(l_scr.shape, F32)
        acc_scr[...] = jnp.zeros(acc_scr.shape, F32)

    k = k_ref[...]
    v = v_ref[...]
    for i in range(2):
        s = lax.dot_general(qm_scr[i], k, (((1,), (1,)), ((), ())), preferred_element_type=F32)
        m_prev = m_scr[i]
        m_new = jnp.maximum(m_prev, jnp.max(s, axis=1, keepdims=True))
        alpha = jnp.exp(m_prev - m_new)
        p = jnp.exp(s - m_new)
        l_scr[i] = alpha * l_scr[i] + jnp.sum(p, axis=1, keepdims=True)
        acc_scr[i] = alpha * acc_scr[i] + jnp.dot(p.astype(BF16), v, preferred_element_type=F32)
        m_scr[i] = m_new

    @pl.when(kb == pl.num_programs(3) - 1)
    def _():
        lam = (jnp.exp(jnp.sum(lq1_ref[...] * lk1_ref[...], axis=1, keepdims=True))
               - jnp.exp(jnp.sum(lq2_ref[...] * lk2_ref[...], axis=1, keepdims=True)) + lam_init)
        o = acc_scr[0] / l_scr[0] - lam * (acc_scr[1] / l_scr[1])
        o = _rms(o, subln_ref[...]) * (1.0 - lam_init)
        o_ref[...] = o.astype(o_ref.dtype)


def _attention(q, k, v, lq1, lk1, lq2, lk2, subln, lam_init, tq=512, tk=512):
    n_seq, n_heads, seq_len, _ = q.shape
    vec = lambda a: pl.BlockSpec(a.shape, lambda s, h, i, j: (0, 0))
    return pl.pallas_call(
        functools.partial(_attn_kernel, lam_init=lam_init),
        grid=(n_seq, n_heads, seq_len // tq, seq_len // tk),
        in_specs=[
            vec(lq1), vec(lk1), vec(lq2), vec(lk2), vec(subln),
            pl.BlockSpec((None, None, tq, LANES), lambda s, h, i, j: (s, h, i, 0)),
            pl.BlockSpec((None, None, tk, LANES), lambda s, h, i, j: (s, h, j, 0)),
            pl.BlockSpec((None, None, tk, LANES), lambda s, h, i, j: (s, h, j, 0)),
        ],
        out_specs=pl.BlockSpec((None, tq, LANES), lambda s, h, i, j: (s, i, h)),
        out_shape=jax.ShapeDtypeStruct((n_seq, seq_len, n_heads * V_HEAD_DIM), BF16),
        scratch_shapes=[
            pltpu.VMEM((2, tq, LANES), BF16),
            pltpu.VMEM((2, tq, 1), F32),
            pltpu.VMEM((2, tq, 1), F32),
            pltpu.VMEM((2, tq, V_HEAD_DIM), F32),
        ],
        compiler_params=_cparams(("parallel", "parallel", "parallel", "arbitrary")),
        name="diff_attention",
    )(lq1, lk1, lq2, lk2, subln, q, k, v)


def _out_proj_kernel(z_ref, a_ref, gh_ref, wt_ref, wb_ref, gp_ref, x_ref, o_ref):
    hn = _rms(z_ref[...], gh_ref[...]).astype(BF16)
    mix = (jnp.dot(hn, wt_ref[...], preferred_element_type=F32)
           + jnp.dot(a_ref[...], wb_ref[...], preferred_element_type=F32))
    o_ref[...] = x_ref[...] + _rms(mix, gp_ref[...])


def _out_proj(z, at, g_hy, w_out, g_post, x, tm=512):
    t_tok = x.shape[0]
    row = lambda w: pl.BlockSpec((tm, w), lambda i: (i, 0))
    return pl.pallas_call(
        _out_proj_kernel,
        grid=(t_tok // tm,),
        in_specs=[
            row(HYENA_WIDTH), row(ATTN_WIDTH),
            pl.BlockSpec((1, HYENA_WIDTH), lambda i: (0, 0)),
            pl.BlockSpec((HYENA_WIDTH, D_MODEL), lambda i: (0, 0)),
            pl.BlockSpec((ATTN_WIDTH, D_MODEL), lambda i: (1, 0)),
            pl.BlockSpec((1, D_MODEL), lambda i: (0, 0)),
            row(D_MODEL),
        ],
        out_specs=row(D_MODEL),
        out_shape=jax.ShapeDtypeStruct(x.shape, F32),
        compiler_params=_cparams(("parallel",)),
        name="out_proj",
    )(z, at, g_hy, w_out, w_out, g_post, x)


FFN_TF = 512


def _ffn_kernel(x_ref, gpre_ref, wg_ref, wu_ref, wd_ref, gpost_ref, o_ref, h_scr, acc_scr):
    f = pl.program_id(1)

    @pl.when(f == 0)
    def _():
        h_scr[...] = _rms(x_ref[...], gpre_ref[...]).astype(BF16)
        acc_scr[...] = jnp.zeros(acc_scr.shape, F32)

    h = h_scr[...]
    gate = jnp.dot(h, wg_ref[...], preferred_element_type=F32)
    up = jnp.dot(h, wu_ref[...], preferred_element_type=F32)
    act = (gate * jax.nn.sigmoid(gate) * up).astype(BF16)
    acc_scr[...] += jnp.dot(act, wd_ref[...], preferred_element_type=F32)

    @pl.when(f == pl.num_programs(1) - 1)
    def _():
        o_ref[...] = x_ref[...] + _rms(acc_scr[...], gpost_ref[...])


def _ffn(x, g_pre, w_gate, w_up, w_down, g_post, tm=512):
    t_tok = x.shape[0]
    return pl.pallas_call(
        _ffn_kernel,
        grid=(t_tok // tm, D_FF // FFN_TF),
        in_specs=[
            pl.BlockSpec((tm, D_MODEL), lambda i, f: (i, 0)),
            pl.BlockSpec((1, D_MODEL), lambda i, f: (0, 0)),
            pl.BlockSpec((D_MODEL, FFN_TF), lambda i, f: (0, f)),
            pl.BlockSpec((D_MODEL, FFN_TF), lambda i, f: (0, f)),
            pl.BlockSpec((FFN_TF, D_MODEL), lambda i, f: (f, 0)),
            pl.BlockSpec((1, D_MODEL), lambda i, f: (0, 0)),
        ],
        out_specs=pl.BlockSpec((tm, D_MODEL), lambda i, f: (i, 0)),
        out_shape=jax.ShapeDtypeStruct(x.shape, F32),
        scratch_shapes=[pltpu.VMEM((tm, D_MODEL), BF16), pltpu.VMEM((tm, D_MODEL), F32)],
        compiler_params=_cparams(("parallel", "arbitrary")),
        name="ffn",
    )(x, g_pre, w_gate, w_up, w_down, g_post)


def _rope_tables(seq_len):
    inv_freq = 1.0 / (ROPE_THETA ** (jnp.arange(0, HEAD_DIM, 2, dtype=F32) / HEAD_DIM))
    ang = jnp.arange(seq_len, dtype=F32)[:, None] * inv_freq[None, :]
    cos, sin = jnp.cos(ang), jnp.sin(ang)
    zero = jnp.zeros_like(sin)
    reps = LANES // HEAD_DIM
    cos_t = jnp.tile(jnp.concatenate([cos, cos], axis=1), (1, reps))
    sa_t = jnp.tile(jnp.concatenate([-sin, zero], axis=1), (1, reps))
    sb_t = jnp.tile(jnp.concatenate([zero, sin], axis=1), (1, reps))
    return cos_t, sa_t, sb_t


def _hyena_conv(u_v, v_off, xg, xg_off, kf, bias, tabs):
    m_pair, _, m_inv, fb, fbi = tabs
    a1 = _dft_a(u_v, m_pair, v_off)
    p, _, r, nbc, sl, c = a1.shape
    g = _dft_b_conv(a1.reshape(p, 2, r, nbc * sl, c), kf, fb, fbi)
    return _dft_c_gate(g.reshape(a1.shape), m_inv, xg, xg_off, u_v, v_off, bias)


def _layer(x, li, n_seq, seq_len, rope, tabs, zfeat, delta, w):
    (norm_mix_pre, norm_mix_post, norm_ffn_pre, norm_ffn_post, w_in, conv_w, conv_b,
     filt_w1, filt_b1, filt_f1, filt_w2, filt_b2, filt_f2, filt_w3, filt_b3, filt_f3, filt_w4,
     hyena_bias, hyena_norm, lam_q1, lam_k1, lam_q2, lam_k2, subln,
     w_out, w_gate, w_up, w_down) = w
    lam_init = 0.8 - 0.6 * math.exp(-0.3 * li)
    row = lambda a: a[li].reshape(1, -1)
    r = DFT_RADIX
    nbc = r // SUBLANES
    n_pair = n_seq // 2

    hy, q, k, v = _in_proj(x, row(norm_mix_pre), w_in[li], *rope, n_seq, seq_len)

    cw = jnp.pad(conv_w[li], ((0, SUBLANES - conv_w.shape[1]), (0, 0)))
    u = _short_conv(hy.reshape(n_seq, seq_len, HYENA_IN), cw, row(conv_b))
    u6 = u.reshape(n_pair, 2, r // 2, nbc, SUBLANES, HYENA_IN)
    pad_w1 = jnp.pad(filt_w1[li], ((0, FILTER_HIDDEN - FILTER_EMB), (0, 0)))
    kt = _filters(zfeat, pad_w1, row(filt_b1), row(filt_f1), filt_w2[li], row(filt_b2), row(filt_f2),
                  filt_w3[li], row(filt_b3), row(filt_f3), filt_w4[li], delta, seq_len)
    ka = _dft_a(kt.reshape(2, r, nbc, SUBLANES, HYENA_WIDTH), tabs[1], 0)
    kf = _dft_b_fwd(ka.reshape(2, 2, r, r, HYENA_WIDTH), tabs[3], 1.0 / (2 * seq_len))
    z1 = _hyena_conv(u6, 2 * HYENA_WIDTH, u6, 0, kf[0], hyena_bias[li, 0].reshape(1, -1), tabs)
    z2 = _hyena_conv(z1, 0, u6, HYENA_WIDTH, kf[1], hyena_bias[li, 1].reshape(1, -1), tabs)

    at = _attention(q, k, v, row(lam_q1), row(lam_k1), row(lam_q2), row(lam_k2), row(subln), lam_init)

    x = _out_proj(z2.reshape(n_seq * seq_len, HYENA_WIDTH), at.reshape(n_seq * seq_len, ATTN_WIDTH),
                  row(hyena_norm), w_out[li], row(norm_mix_post), x)
    return _ffn(x, row(norm_ffn_pre), w_gate[li], w_up[li], w_down[li], row(norm_ffn_post))


def kernel(x_prompt, x_sample, norm_mix_pre, norm_mix_post, norm_ffn_pre, norm_ffn_post, w_in, conv_w, conv_b, filt_w1, filt_b1, filt_f1, filt_w2, filt_b2, filt_f2, filt_w3, filt_b3, filt_f3, filt_w4, hyena_bias, hyena_norm, lam_q1, lam_k1, lam_q2, lam_k2, subln, w_out, w_gate, w_up, w_down):
    assert x_prompt.shape[1:] == x_sample.shape[1:]
    n_p, seq_len, d = x_prompt.shape
    n_s = x_sample.shape[0]
    n_seq = n_p + n_s
    assert n_p % 2 == 0 and n_s % 2 == 0 and seq_len == DFT_RADIX * DFT_RADIX // 2 and d == D_MODEL
    x = jnp.concatenate([x_prompt, x_sample], axis=0).reshape(n_seq * seq_len, d)
    w = (norm_mix_pre, norm_mix_post, norm_ffn_pre, norm_ffn_post, w_in.astype(BF16), conv_w, conv_b,
         filt_w1, filt_b1, filt_f1, filt_w2, filt_b2, filt_f2, filt_w3, filt_b3, filt_f3, filt_w4,
         hyena_bias, hyena_norm, lam_q1, lam_k1, lam_q2, lam_k2, subln,
         w_out.astype(BF16), w_gate.astype(BF16), w_up.astype(BF16), w_down.astype(BF16))
    rope = _rope_tables(seq_len)
    tabs = _dft_tables(seq_len)
    zfeat = _filter_features(seq_len)
    delta = jnp.abs(jnp.linspace(MIN_DECAY, MAX_DECAY, HYENA_WIDTH, dtype=F32)).reshape(1, -1)
    for li in range(DEPTH):
        x = _layer(x, li, n_seq, seq_len, rope, tabs, zfeat, delta, w)
    y = x.reshape(n_seq, seq_len, d)
    return (y[:n_p], y[n_p:])
```

```python
import functools
import math

import jax
import jax.numpy as jnp
from jax import lax
from jax.experimental import pallas as pl
from jax.experimental.pallas import tpu as pltpu

F32 = jnp.float32
BF16 = jnp.bfloat16

D_MODEL = 2048
DEPTH = 4
HYENA_WIDTH = 1024
ATTN_WIDTH = 1024
N_HEADS = 8
HEAD_DIM = 64
V_HEAD_DIM = 128
HYENA_IN = 3 * HYENA_WIDTH
IN_WIDTH = HYENA_IN + 3 * ATTN_WIDTH
FILTER_EMB = 33
FILTER_BANDS = 16
FILTER_HIDDEN = 64
DECAY_TARGET = 1e-2
MIN_DECAY = math.log(DECAY_TARGET) / 0.3
MAX_DECAY = math.log(DECAY_TARGET) / 1.5
D_FF = 5632
ROPE_THETA = 10000.0
EPS = 1e-6
LOG2E = math.log2(math.e)

LANES = 128
SUBLANES = 8
DFT_RADIX = 128
VMEM_LIMIT = 56 * 1024 * 1024


def _cparams(sem):
    return pltpu.CompilerParams(dimension_semantics=sem, vmem_limit_bytes=VMEM_LIMIT)


def _rms(xf, g):
    return xf * lax.rsqrt(jnp.mean(xf * xf, axis=-1, keepdims=True) + EPS) * g


IN_TN = 1024
HY_TILES = HYENA_IN // IN_TN


def _in_proj_kernel(x_ref, g_ref, w_ref, cos_ref, sa_ref, sb_ref,
                    hy_ref, q_ref, k_ref, v_ref, h_scr):
    j = pl.program_id(1)

    @pl.when(j == 0)
    def _():
        h_scr[...] = _rms(x_ref[...], g_ref[...]).astype(BF16)

    res = jnp.dot(h_scr[...], w_ref[...], preferred_element_type=F32)

    @pl.when(j < HY_TILES)
    def _():
        hy_ref[...] = res

    def rope(sl):
        return (sl * cos_ref[...] + pltpu.roll(sl, LANES - HEAD_DIM // 2, 1) * sa_ref[...]
                + pltpu.roll(sl, HEAD_DIM // 2, 1) * sb_ref[...])

    @pl.when(j == HY_TILES)
    def _():
        for h in range(N_HEADS):
            sl = res[:, h * LANES:(h + 1) * LANES]
            q_ref[h] = (rope(sl) * (HEAD_DIM ** -0.5 * LOG2E)).T.astype(BF16)

    @pl.when(j == HY_TILES + 1)
    def _():
        for h in range(N_HEADS):
            k_ref[h] = rope(res[:, h * LANES:(h + 1) * LANES]).astype(BF16)

    @pl.when(j == HY_TILES + 2)
    def _():
        for h in range(N_HEADS):
            v_ref[h] = res[:, h * LANES:(h + 1) * LANES].T.astype(BF16)


def _in_proj(x, g, w, cos_t, sa_t, sb_t, n_seq, seq_len, tm=512):
    t_tok = x.shape[0]
    lt = seq_len // tm
    head_spec = pl.BlockSpec((None, N_HEADS, tm, LANES), lambda i, j: (i // lt, 0, i % lt, 0))
    tile_spec = pl.BlockSpec((None, N_HEADS, None, LANES, tm), lambda i, j: (i // lt, 0, i % lt, 0, 0))
    tab_spec = pl.BlockSpec((tm, LANES), lambda i, j: (i % lt, 0))
    head_shape = jax.ShapeDtypeStruct((n_seq, N_HEADS, seq_len, LANES), BF16)
    tile_shape = jax.ShapeDtypeStruct((n_seq, N_HEADS, lt, LANES, tm), BF16)
    return pl.pallas_call(
        _in_proj_kernel,
        grid=(t_tok // tm, IN_WIDTH // IN_TN),
        in_specs=[
            pl.BlockSpec((tm, D_MODEL), lambda i, j: (i, 0)),
            pl.BlockSpec((1, D_MODEL), lambda i, j: (0, 0)),
            pl.BlockSpec((D_MODEL, IN_TN), lambda i, j: (0, j)),
            tab_spec, tab_spec, tab_spec,
        ],
        out_specs=[
            pl.BlockSpec((tm, IN_TN), lambda i, j: (i, jnp.minimum(j, HY_TILES - 1))),
            tile_spec, head_spec, tile_spec,
        ],
        out_shape=[jax.ShapeDtypeStruct((t_tok, HYENA_IN), F32), tile_shape, head_shape, tile_shape],
        scratch_shapes=[pltpu.VMEM((tm, D_MODEL), BF16)],
        compiler_params=_cparams(("parallel", "arbitrary")),
        name="in_proj",
    )(x, g, w, cos_t, sa_t, sb_t)


def _short_conv_kernel(x_ref, w_ref, b_ref, o_ref):
    x = x_ref[...]
    n = x.shape[0]
    row = lax.broadcasted_iota(jnp.int32, x.shape, 0)
    prev = jnp.where(row == 0, 0.0, pltpu.roll(x, 1, 0))
    nxt = jnp.where(row == n - 1, 0.0, pltpu.roll(x, n - 1, 0))
    o_ref[...] = b_ref[...] + prev * w_ref[0:1, :] + x * w_ref[1:2, :] + nxt * w_ref[2:3, :]


def _short_conv(hy, w, b):
    n_seq, seq_len, width = hy.shape
    return pl.pallas_call(
        _short_conv_kernel,
        grid=(n_seq, width // LANES),
        in_specs=[
            pl.BlockSpec((None, seq_len, LANES), lambda s, c: (s, 0, c)),
            pl.BlockSpec((SUBLANES, LANES), lambda s, c: (0, c)),
            pl.BlockSpec((1, LANES), lambda s, c: (0, c)),
        ],
        out_specs=pl.BlockSpec((None, seq_len, LANES), lambda s, c: (s, 0, c)),
        out_shape=jax.ShapeDtypeStruct(hy.shape, F32),
        compiler_params=_cparams(("parallel", "parallel")),
        name="short_conv",
    )(hy, w, b)


def _filter_kernel(z_ref, w1_ref, b1_ref, f1_ref, w2_ref, b2_ref, f2_ref,
                   w3_ref, b3_ref, f3_ref, w4_ref, delta_ref, o_ref, *, seq_len):
    hp = lax.Precision.HIGHEST
    z = z_ref[...]
    h = jnp.sin(f1_ref[...] * (jnp.dot(z, w1_ref[...], precision=hp, preferred_element_type=F32) + b1_ref[...]))
    h = jnp.sin(f2_ref[...] * (jnp.dot(h, w2_ref[...], precision=hp, preferred_element_type=F32) + b2_ref[...]))
    h = jnp.sin(f3_ref[...] * (jnp.dot(h, w3_ref[...], precision=hp, preferred_element_type=F32) + b3_ref[...]))
    rb = z.shape[0]
    n = pl.program_id(0) * rb + lax.broadcasted_iota(jnp.int32, (rb, 1), 0)
    m_fwd = (n < seq_len).astype(F32)
    m_bwd = jnp.logical_or(n == 0, n > seq_len).astype(F32)
    decay = jnp.exp(-z[:, 0:1] * delta_ref[...])
    cw = HYENA_WIDTH
    for order in range(2):
        base = order * 2 * cw
        fwd = jnp.dot(h, w4_ref[:, base:base + cw], precision=hp, preferred_element_type=F32)
        bwd = jnp.dot(h, w4_ref[:, base + cw:base + 2 * cw], precision=hp, preferred_element_type=F32)
        o_ref[order] = (m_fwd * fwd + m_bwd * bwd) * decay


def _filters(zfeat, w1, b1, f1, w2, b2, f2, w3, b3, f3, w4, delta, seq_len, rb=512):
    n2 = zfeat.shape[0]
    full = lambda a: pl.BlockSpec(a.shape, lambda i: (0,) * a.ndim)
    args = (w1, b1, f1, w2, b2, f2, w3, b3, f3, w4, delta)
    return pl.pallas_call(
        functools.partial(_filter_kernel, seq_len=seq_len),
        grid=(n2 // rb,),
        in_specs=[pl.BlockSpec((rb, zfeat.shape[1]), lambda i: (i, 0))] + [full(a) for a in args],
        out_specs=pl.BlockSpec((2, rb, HYENA_WIDTH), lambda i: (0, i, 0)),
        out_shape=jax.ShapeDtypeStruct((2, n2, HYENA_WIDTH), F32),
        compiler_params=_cparams(("parallel",)),
        name="hyena_filter",
    )(zfeat, *args)


DFT_CB = 256
DFT_KK = 8


def _dft_a_kernel(u_ref, m_ref, o_ref):
    cb = u_ref.shape[-1]
    x = u_ref[...].reshape(DFT_RADIX * SUBLANES, cb).astype(BF16)
    r = jnp.dot(m_ref[...], x, preferred_element_type=F32)
    o_ref[...] = r.reshape(o_ref.shape)


def _dft_a(u6, m_a, ch_off):
    g = u6.shape[0]
    lead = u6.shape[1:-3]
    nbc = u6.shape[-3]
    cout = HYENA_WIDTH
    blk_in = (None,) + lead + (None, SUBLANES, DFT_CB)
    zeros = (0,) * len(lead)
    return pl.pallas_call(
        _dft_a_kernel,
        grid=(nbc, g, cout // DFT_CB),
        in_specs=[
            pl.BlockSpec(blk_in, lambda bc, p, c: (p,) + zeros + (bc, 0, c + ch_off // DFT_CB)),
            pl.BlockSpec((None,) + m_a.shape[1:], lambda bc, p, c: (bc, 0, 0)),
        ],
        out_specs=pl.BlockSpec((None, 2, DFT_RADIX, None, SUBLANES, DFT_CB),
                               lambda bc, p, c: (p, 0, 0, bc, 0, c)),
        out_shape=jax.ShapeDtypeStruct((g, 2, DFT_RADIX, nbc, SUBLANES, cout), F32),
        compiler_params=_cparams(("arbitrary", "arbitrary", "arbitrary")),
        name="dft_stage_a",
    )(u6, m_a)


def _dft_b_conv_kernel(a_ref, kf_ref, fb_ref, fbi_ref, g_ref):
    r = DFT_RADIX
    for i in range(DFT_KK):
        s = jnp.concatenate([a_ref[0, i], a_ref[1, i]], axis=0).astype(BF16)
        x = jnp.dot(fb_ref[...], s, preferred_element_type=F32)
        xr, xi = x[:r], x[r:]
        kr, ki = kf_ref[0, i], kf_ref[1, i]
        y = jnp.concatenate([xr * kr - xi * ki, xr * ki + xi * kr], axis=0).astype(BF16)
        gg = jnp.dot(fbi_ref[...], y, preferred_element_type=F32)
        g_ref[0, i] = gg[:r]
        g_ref[1, i] = gg[r:]


def _dft_b_conv(a1, kf, fb, fbi):
    p, _, r, _, c = a1.shape
    blk = pl.BlockSpec((None, 2, DFT_KK, r, DFT_CB), lambda k, cc, pp: (pp, 0, k, 0, cc))
    mat = pl.BlockSpec((2 * r, 2 * r), lambda k, cc, pp: (0, 0))
    return pl.pallas_call(
        _dft_b_conv_kernel,
        grid=(r // DFT_KK, c // DFT_CB, p),
        in_specs=[blk, pl.BlockSpec((2, DFT_KK, r, DFT_CB), lambda k, cc, pp: (0, k, 0, cc)), mat, mat],
        out_specs=blk,
        out_shape=jax.ShapeDtypeStruct(a1.shape, F32),
        compiler_params=_cparams(("parallel", "parallel", "arbitrary")),
        name="dft_stage_b_conv",
    )(a1, kf, fb, fbi)


def _dft_b_fwd_kernel(a_ref, fb_ref, o_ref, *, scale):
    r = DFT_RADIX
    for i in range(DFT_KK):
        s = jnp.concatenate([a_ref[0, i], a_ref[1, i]], axis=0).astype(BF16)
        x = jnp.dot(fb_ref[...], s, preferred_element_type=F32) * scale
        o_ref[0, i] = x[:r]
        o_ref[1, i] = x[r:]


def _dft_b_fwd(a1, fb, scale):
    p, _, r, _, c = a1.shape
    blk = pl.BlockSpec((None, 2, DFT_KK, r, DFT_CB), lambda k, cc, pp: (pp, 0, k, 0, cc))
    return pl.pallas_call(
        functools.partial(_dft_b_fwd_kernel, scale=scale),
        grid=(r // DFT_KK, c // DFT_CB, p),
        in_specs=[blk, pl.BlockSpec((2 * r, 2 * r), lambda k, cc, pp: (0, 0))],
        out_specs=blk,
        out_shape=jax.ShapeDtypeStruct(a1.shape, F32),
        compiler_params=_cparams(("parallel", "parallel", "arbitrary")),
        name="dft_stage_b_filter",
    )(a1, fb)


def _dft_c_gate_kernel(g_ref, m_ref, x_ref, v_ref, bias_ref, o_ref):
    cb = g_ref.shape[-1]
    g = g_ref[...].reshape(2 * DFT_RADIX * SUBLANES, cb).astype(BF16)
    y = jnp.dot(m_ref[...], g, preferred_element_type=F32).reshape(o_ref.shape)
    o_ref[...] = x_ref[...] * (y + v_ref[...] * bias_ref[...])


def _dft_c_gate(g6, m_c, xg, xg_off, v, v_off, bias):
    p, _, r, nbc, _, c = g6.shape
    half = r // 2
    tspec = lambda off: pl.BlockSpec((None, 2, half, None, SUBLANES, DFT_CB),
                                     lambda bc, pp, cc: (pp, 0, 0, bc, 0, cc + off // DFT_CB))
    return pl.pallas_call(
        _dft_c_gate_kernel,
        grid=(nbc, p, c // DFT_CB),
        in_specs=[
            pl.BlockSpec((None, 2, r, None, SUBLANES, DFT_CB), lambda bc, pp, cc: (pp, 0, 0, bc, 0, cc)),
            pl.BlockSpec((None,) + m_c.shape[1:], lambda bc, pp, cc: (bc, 0, 0)),
            tspec(xg_off), tspec(v_off),
            pl.BlockSpec((1, DFT_CB), lambda bc, pp, cc: (0, cc)),
        ],
        out_specs=tspec(0),
        out_shape=jax.ShapeDtypeStruct((p, 2, half, nbc, SUBLANES, c), F32),
        compiler_params=_cparams(("arbitrary", "arbitrary", "arbitrary")),
        name="dft_stage_c_gate",
    )(g6, m_c, xg, v, bias)


def _dft_tables(seq_len):
    r = DFT_RADIX
    n = 2 * seq_len
    nbc = r // SUBLANES
    i32 = jnp.int32
    bc = jnp.arange(nbc, dtype=i32).reshape(nbc, 1, 1, 1)
    k1 = jnp.arange(r, dtype=i32).reshape(1, r, 1, 1)
    jj = jnp.arange(SUBLANES, dtype=i32).reshape(1, 1, SUBLANES, 1)
    aa = jnp.arange(r, dtype=i32).reshape(1, 1, 1, r)
    ang = ((k1 * (r * aa + SUBLANES * bc + jj)) % n).astype(F32) * (2.0 * math.pi / n)
    c, s = jnp.cos(ang), jnp.sin(ang)
    eye = jnp.eye(SUBLANES, dtype=F32)

    def expand(t):
        return t[..., None] * eye[None, None, :, None, :]

    half = r // 2
    ch, sh = c[..., :half], s[..., :half]
    re_rows = jnp.stack([expand(ch), expand(sh)], axis=3)
    im_rows = jnp.stack([expand(-sh), expand(ch)], axis=3)
    m_pair = jnp.stack([re_rows, im_rows], axis=1).reshape(nbc, 2 * r * SUBLANES, r * SUBLANES)
    m_real = jnp.stack([expand(c), expand(-s)], axis=1).reshape(nbc, 2 * r * SUBLANES, r * SUBLANES)
    m_pair = m_pair.astype(BF16)
    m_inv = jnp.swapaxes(m_pair, 1, 2)
    kb = jnp.arange(r, dtype=i32)
    angb = ((kb[:, None] * kb[None, :]) % r).astype(F32) * (2.0 * math.pi / r)
    cb_, sb_ = jnp.cos(angb), jnp.sin(angb)
    fb = jnp.block([[cb_, sb_], [-sb_, cb_]]).astype(BF16)
    fbi = jnp.block([[cb_, -sb_], [sb_, cb_]]).astype(BF16)
    return m_pair, m_real.astype(BF16), m_inv, fb, fbi


def _filter_features(seq_len):
    n2 = 2 * seq_len
    n = jnp.arange(n2)
    pos = jnp.where(n <= seq_len, n, n2 - n)
    pos = jnp.minimum(pos, seq_len - 1)
    t_lin = jnp.linspace(0.0, 1.0, seq_len, dtype=F32)
    omega = 2.0 * math.pi * jnp.arange(seq_len, dtype=F32) / seq_len
    bands = jnp.linspace(1e-4, FILTER_BANDS - 1, FILTER_BANDS, dtype=F32)
    phase = omega[:, None] * bands[None, :]
    z = jnp.concatenate([t_lin[:, None], jnp.cos(phase), -jnp.sin(phase)], axis=-1)
    z = jnp.pad(z, ((0, 0), (0, FILTER_HIDDEN - FILTER_EMB)))
    return z[pos]


ATTN_Q_TILES = 2


def _attn_kernel(lq1_ref, lk1_ref, lq2_ref, lk2_ref, subln_ref, qt_ref, k_ref, vt_ref, o_ref,
                 acc_scr, s_scr, *, lam_init):
    qt = jnp.concatenate([qt_ref[t] for t in range(qt_ref.shape[0])], axis=1)
    tq = qt.shape[1]
    tc = vt_ref.shape[2]
    row = lax.broadcasted_iota(jnp.int32, qt.shape, 0)
    zero = jnp.zeros_like(qt)
    qm = (jnp.where(row < HEAD_DIM, qt, zero), jnp.where(row >= HEAD_DIM, qt, zero))
    acc_scr[...] = jnp.zeros(acc_scr.shape, F32)
    n_chunks = vt_ref.shape[0]

    def scores(c, slot):
        k_c = k_ref[pl.ds(pl.multiple_of(c * tc, tc), tc), :]
        cms = []
        for i in range(2):
            st = jnp.dot(k_c, qm[i], preferred_element_type=F32)
            s_scr[slot, i] = st
            cms.append(jnp.max(st, axis=0, keepdims=True))
        return tuple(cms)

    def softmax_pv(c, slot, cms, ms, ls):
        vt_c = vt_ref[c]
        new_m, new_l = [], []
        for i in range(2):
            m_new = jnp.maximum(ms[i], cms[i])
            alpha = jnp.exp2(ms[i] - m_new)
            p = jnp.exp2(s_scr[slot, i] - m_new)
            new_l.append(alpha * ls[i] + jnp.sum(p, axis=0, keepdims=True))
            acc_scr[i] = alpha * acc_scr[i] + jnp.dot(vt_c, p.astype(BF16), preferred_element_type=F32)
            new_m.append(m_new)
        return tuple(new_m), tuple(new_l)

    def pair(j, carry):
        cm, ms, ls = carry
        c = 2 * j
        cm1 = scores(c + 1, 1)
        ms, ls = softmax_pv(c, 0, cm, ms, ls)
        cm2 = scores(c + 2, 0)
        ms, ls = softmax_pv(c + 1, 1, cm1, ms, ls)
        return cm2, ms, ls

    neg = jnp.full((1, tq), -jnp.inf, F32)
    zl = jnp.zeros((1, tq), F32)
    cm, ms, ls = lax.fori_loop(0, n_chunks // 2 - 1, pair, (scores(0, 0), (neg, neg), (zl, zl)))
    cm1 = scores(n_chunks - 1, 1)
    ms, ls = softmax_pv(n_chunks - 2, 0, cm, ms, ls)
    ms, ls = softmax_pv(n_chunks - 1, 1, cm1, ms, ls)

    lam = (jnp.exp(jnp.sum(lq1_ref[...] * lk1_ref[...], axis=1, keepdims=True))
           - jnp.exp(jnp.sum(lq2_ref[...] * lk2_ref[...], axis=1, keepdims=True)) + lam_init)
    ot = acc_scr[0] * (1.0 / ls[0]) - lam * (acc_scr[1] * (1.0 / ls[1]))
    o = _rms(ot.T, subln_ref[...]) * (1.0 - lam_init)
    o_ref[...] = o.astype(o_ref.dtype)


def _attention(qt, k, vt, lq1, lk1, lq2, lk2, subln, lam_init):
    n_seq, n_heads, n_tiles, _, tc = qt.shape
    seq_len = k.shape[2]
    tq = ATTN_Q_TILES * tc
    vec = lambda a: pl.BlockSpec(a.shape, lambda s, h, i: (0, 0))
    return pl.pallas_call(
        functools.partial(_attn_kernel, lam_init=lam_init),
        grid=(n_seq, n_heads, seq_len // tq),
        in_specs=[
            vec(lq1), vec(lk1), vec(lq2), vec(lk2), vec(subln),
            pl.BlockSpec((None, None, ATTN_Q_TILES, LANES, tc), lambda s, h, i: (s, h, i, 0, 0)),
            pl.BlockSpec((None, None, seq_len, LANES), lambda s, h, i: (s, h, 0, 0)),
            pl.BlockSpec((None, None, n_tiles, LANES, tc), lambda s, h, i: (s, h, 0, 0, 0)),
        ],
        out_specs=pl.BlockSpec((None, tq, LANES), lambda s, h, i: (s, i, h)),
        out_shape=jax.ShapeDtypeStruct((n_seq, seq_len, n_heads * V_HEAD_DIM), BF16),
        scratch_shapes=[pltpu.VMEM((2, V_HEAD_DIM, tq), F32), pltpu.VMEM((2, 2, tc, tq), F32)],
        compiler_params=_cparams(("parallel", "parallel", "arbitrary")),
        name="diff_attention",
    )(lq1, lk1, lq2, lk2, subln, qt, k, vt)


def _out_proj_kernel(z_ref, a_ref, gh_ref, wt_ref, wb_ref, gp_ref, x_ref, o_ref):
    hn = _rms(z_ref[...], gh_ref[...]).astype(BF16)
    mix = (jnp.dot(hn, wt_ref[...], preferred_element_type=F32)
           + jnp.dot(a_ref[...], wb_ref[...], preferred_element_type=F32))
    o_ref[...] = x_ref[...] + _rms(mix, gp_ref[...])


def _out_proj(z, at, g_hy, w_out, g_post, x, tm=512):
    t_tok = x.shape[0]
    row = lambda w: pl.BlockSpec((tm, w), lambda i: (i, 0))
    return pl.pallas_call(
        _out_proj_kernel,
        grid=(t_tok // tm,),
        in_specs=[
            row(HYENA_WIDTH), row(ATTN_WIDTH),
            pl.BlockSpec((1, HYENA_WIDTH), lambda i: (0, 0)),
            pl.BlockSpec((HYENA_WIDTH, D_MODEL), lambda i: (0, 0)),
            pl.BlockSpec((ATTN_WIDTH, D_MODEL), lambda i: (1, 0)),
            pl.BlockSpec((1, D_MODEL), lambda i: (0, 0)),
            row(D_MODEL),
        ],
        out_specs=row(D_MODEL),
        out_shape=jax.ShapeDtypeStruct(x.shape, F32),
        compiler_params=_cparams(("parallel",)),
        name="out_proj",
    )(z, at, g_hy, w_out, w_out, g_post, x)


FFN_TF = 512


def _ffn_kernel(x_ref, gpre_ref, wg_ref, wu_ref, wd_ref, gpost_ref, o_ref, h_scr, acc_scr):
    f = pl.program_id(1)

    @pl.when(f == 0)
    def _():
        h_scr[...] = _rms(x_ref[...], gpre_ref[...]).astype(BF16)
        acc_scr[...] = jnp.zeros(acc_scr.shape, F32)

    h = h_scr[...]
    gate = jnp.dot(h, wg_ref[...], preferred_element_type=F32)
    up = jnp.dot(h, wu_ref[...], preferred_element_type=F32)
    act = (gate * jax.nn.sigmoid(gate) * up).astype(BF16)
    acc_scr[...] += jnp.dot(act, wd_ref[...], preferred_element_type=F32)

    @pl.when(f == pl.num_programs(1) - 1)
    def _():
        o_ref[...] = x_ref[...] + _rms(acc_scr[...], gpost_ref[...])


def _ffn(x, g_pre, w_gate, w_up, w_down, g_post, tm=512):
    t_tok = x.shape[0]
    return pl.pallas_call(
        _ffn_kernel,
        grid=(t_tok // tm, D_FF // FFN_TF),
        in_specs=[
            pl.BlockSpec((tm, D_MODEL), lambda i, f: (i, 0)),
            pl.BlockSpec((1, D_MODEL), lambda i, f: (0, 0)),
            pl.BlockSpec((D_MODEL, FFN_TF), lambda i, f: (0, f)),
            pl.BlockSpec((D_MODEL, FFN_TF), lambda i, f: (0, f)),
            pl.BlockSpec((FFN_TF, D_MODEL), lambda i, f: (f, 0)),
            pl.BlockSpec((1, D_MODEL), lambda i, f: (0, 0)),
        ],
        out_specs=pl.BlockSpec((tm, D_MODEL), lambda i, f: (i, 0)),
        out_shape=jax.ShapeDtypeStruct(x.shape, F32),
        scratch_shapes=[pltpu.VMEM((tm, D_MODEL), BF16), pltpu.VMEM((tm, D_MODEL), F32)],
        compiler_params=_cparams(("parallel", "arbitrary")),
        name="ffn",
    )(x, g_pre, w_gate, w_up, w_down, g_post)


def _rope_tables(seq_len):
    inv_freq = 1.0 / (ROPE_THETA ** (jnp.arange(0, HEAD_DIM, 2, dtype=F32) / HEAD_DIM))
    ang = jnp.arange(seq_len, dtype=F32)[:, None] * inv_freq[None, :]
    cos, sin = jnp.cos(ang), jnp.sin(ang)
    zero = jnp.zeros_like(sin)
    reps = LANES // HEAD_DIM
    cos_t = jnp.tile(jnp.concatenate([cos, cos], axis=1), (1, reps))
    sa_t = jnp.tile(jnp.concatenate([-sin, zero], axis=1), (1, reps))
    sb_t = jnp.tile(jnp.concatenate([zero, sin], axis=1), (1, reps))
    return cos_t, sa_t, sb_t


def _hyena_conv(u_v, v_off, xg, xg_off, kf, bias, tabs):
    m_pair, _, m_inv, fb, fbi = tabs
    a1 = _dft_a(u_v, m_pair, v_off)
    p, _, r, nbc, sl, c = a1.shape
    g = _dft_b_conv(a1.reshape(p, 2, r, nbc * sl, c), kf, fb, fbi)
    return _dft_c_gate(g.reshape(a1.shape), m_inv, xg, xg_off, u_v, v_off, bias)


def _layer(x, li, n_seq, seq_len, rope, tabs, zfeat, delta, w):
    (norm_mix_pre, norm_mix_post, norm_ffn_pre, norm_ffn_post, w_in, conv_w, conv_b,
     filt_w1, filt_b1, filt_f1, filt_w2, filt_b2, filt_f2, filt_w3, filt_b3, filt_f3, filt_w4,
     hyena_bias, hyena_norm, lam_q1, lam_k1, lam_q2, lam_k2, subln,
     w_out, w_gate, w_up, w_down) = w
    lam_init = 0.8 - 0.6 * math.exp(-0.3 * li)
    row = lambda a: a[li].reshape(1, -1)
    r = DFT_RADIX
    nbc = r // SUBLANES
    n_pair = n_seq // 2

    hy, q, k, v = _in_proj(x, row(norm_mix_pre), w_in[li], *rope, n_seq, seq_len)

    cw = jnp.pad(conv_w[li], ((0, SUBLANES - conv_w.shape[1]), (0, 0)))
    u = _short_conv(hy.reshape(n_seq, seq_len, HYENA_IN), cw, row(conv_b))
    u6 = u.reshape(n_pair, 2, r // 2, nbc, SUBLANES, HYENA_IN)
    pad_w1 = jnp.pad(filt_w1[li], ((0, FILTER_HIDDEN - FILTER_EMB), (0, 0)))
    kt = _filters(zfeat, pad_w1, row(filt_b1), row(filt_f1), filt_w2[li], row(filt_b2), row(filt_f2),
                  filt_w3[li], row(filt_b3), row(filt_f3), filt_w4[li], delta, seq_len)
    ka = _dft_a(kt.reshape(2, r, nbc, SUBLANES, HYENA_WIDTH), tabs[1], 0)
    kf = _dft_b_fwd(ka.reshape(2, 2, r, r, HYENA_WIDTH), tabs[3], 1.0 / (2 * seq_len))
    z1 = _hyena_conv(u6, 2 * HYENA_WIDTH, u6, 0, kf[0], hyena_bias[li, 0].reshape(1, -1), tabs)
    z2 = _hyena_conv(z1, 0, u6, HYENA_WIDTH, kf[1], hyena_bias[li, 1].reshape(1, -1), tabs)

    at = _attention(q, k, v, row(lam_q1), row(lam_k1), row(lam_q2), row(lam_k2), row(subln), lam_init)

    x = _out_proj(z2.reshape(n_seq * seq_len, HYENA_WIDTH), at.reshape(n_seq * seq_len, ATTN_WIDTH),
                  row(hyena_norm), w_out[li], row(norm_mix_post), x)
    return _ffn(x, row(norm_ffn_pre), w_gate[li], w_up[li], w_down[li], row(norm_ffn_post))


def kernel(x_prompt, x_sample, norm_mix_pre, norm_mix_post, norm_ffn_pre, norm_ffn_post, w_in, conv_w, conv_b, filt_w1, filt_b1, filt_f1, filt_w2, filt_b2, filt_f2, filt_w3, filt_b3, filt_f3, filt_w4, hyena_bias, hyena_norm, lam_q1, lam_k1, lam_q2, lam_k2, subln, w_out, w_gate, w_up, w_down):
    assert x_prompt.shape[1:] == x_sample.shape[1:]
    n_p, seq_len, d = x_prompt.shape
    n_s = x_sample.shape[0]
    n_seq = n_p + n_s
    assert n_p % 2 == 0 and n_s % 2 == 0 and seq_len == DFT_RADIX * DFT_RADIX // 2 and d == D_MODEL
    x = jnp.concatenate([x_prompt, x_sample], axis=0).reshape(n_seq * seq_len, d)
    w = (norm_mix_pre, norm_mix_post, norm_ffn_pre, norm_ffn_post, w_in.astype(BF16), conv_w, conv_b,
         filt_w1, filt_b1, filt_f1, filt_w2, filt_b2, filt_f2, filt_w3, filt_b3, filt_f3, filt_w4,
         hyena_bias, hyena_norm, lam_q1, lam_k1, lam_q2, lam_k2, subln,
         w_out.astype(BF16), w_gate.astype(BF16), w_up.astype(BF16), w_down.astype(BF16))
    rope = _rope_tables(seq_len)
    tabs = _dft_tables(seq_len)
    zfeat = _filter_features(seq_len)
    delta = jnp.abs(jnp.linspace(MIN_DECAY, MAX_DECAY, HYENA_WIDTH, dtype=F32)).reshape(1, -1)
    for li in range(DEPTH):
        x = _layer(x, li, n_seq, seq_len, rope, tabs, zfeat, delta, w)
    y = x.reshape(n_seq, seq_len, d)
    return (y[:n_p], y[n_p:])
```

```python
import functools
import math

import jax
import jax.numpy as jnp
from jax import lax
from jax.experimental import pallas as pl
from jax.experimental.pallas import tpu as pltpu

F32 = jnp.float32
BF16 = jnp.bfloat16

D_MODEL = 2048
DEPTH = 4
HYENA_WIDTH = 1024
ATTN_WIDTH = 1024
N_HEADS = 8
HEAD_DIM = 64
V_HEAD_DIM = 128
HYENA_IN = 3 * HYENA_WIDTH
IN_WIDTH = HYENA_IN + 3 * ATTN_WIDTH
FILTER_EMB = 33
FILTER_BANDS = 16
FILTER_HIDDEN = 64
DECAY_TARGET = 1e-2
MIN_DECAY = math.log(DECAY_TARGET) / 0.3
MAX_DECAY = math.log(DECAY_TARGET) / 1.5
D_FF = 5632
ROPE_THETA = 10000.0
EPS = 1e-6
LOG2E = math.log2(math.e)

LANES = 128
SUBLANES = 8
DFT_RADIX = 128
VMEM_LIMIT = 56 * 1024 * 1024


def _cparams(sem):
    return pltpu.CompilerParams(dimension_semantics=sem, vmem_limit_bytes=VMEM_LIMIT)


def _rms(xf, g):
    return xf * lax.rsqrt(jnp.mean(xf * xf, axis=-1, keepdims=True) + EPS) * g


IN_TN = 1024
HY_TILES = HYENA_IN // IN_TN


def _in_proj_kernel(x_ref, g_ref, w_ref, cos_ref, sa_ref, sb_ref,
                    hy_ref, q_ref, k_ref, v_ref, h_scr):
    j = pl.program_id(1)

    @pl.when(j == 0)
    def _():
        h_scr[...] = _rms(x_ref[...], g_ref[...]).astype(BF16)

    res = jnp.dot(h_scr[...], w_ref[...], preferred_element_type=F32)

    @pl.when(j < HY_TILES)
    def _():
        hy_ref[...] = res

    def rope(sl):
        return (sl * cos_ref[...] + pltpu.roll(sl, LANES - HEAD_DIM // 2, 1) * sa_ref[...]
                + pltpu.roll(sl, HEAD_DIM // 2, 1) * sb_ref[...])

    n_tiles = q_ref.shape[1]
    tt = q_ref.shape[3]

    @pl.when(j == HY_TILES)
    def _():
        for h in range(N_HEADS):
            ro = rope(res[:, h * LANES:(h + 1) * LANES]) * (HEAD_DIM ** -0.5 * LOG2E)
            for t in range(n_tiles):
                q_ref[h, t] = ro[t * tt:(t + 1) * tt].T.astype(BF16)

    @pl.when(j == HY_TILES + 1)
    def _():
        for h in range(N_HEADS):
            k_ref[h] = rope(res[:, h * LANES:(h + 1) * LANES]).astype(BF16)

    @pl.when(j == HY_TILES + 2)
    def _():
        ones = jnp.ones((V_ONES_ROWS, tt), BF16)
        for h in range(N_HEADS):
            for t in range(n_tiles):
                v_ref[h, t, :V_HEAD_DIM] = res[t * tt:(t + 1) * tt, h * LANES:(h + 1) * LANES].T.astype(BF16)
                v_ref[h, t, V_HEAD_DIM:] = ones


ATTN_TILE = 512
V_ONES_ROWS = 16


def _in_proj(x, g, w, cos_t, sa_t, sb_t, n_seq, seq_len, tm=1024):
    t_tok = x.shape[0]
    lt = seq_len // tm
    tt = ATTN_TILE
    head_spec = pl.BlockSpec((None, N_HEADS, tm, LANES), lambda i, j: (i // lt, 0, i % lt, 0))
    tile = lambda rows: pl.BlockSpec((None, N_HEADS, tm // tt, rows, tt), lambda i, j: (i // lt, 0, i % lt, 0, 0))
    tab_spec = pl.BlockSpec((tm, LANES), lambda i, j: (i % lt, 0))
    head_shape = jax.ShapeDtypeStruct((n_seq, N_HEADS, seq_len, LANES), BF16)
    tile_shape = lambda rows: jax.ShapeDtypeStruct((n_seq, N_HEADS, seq_len // tt, rows, tt), BF16)
    v_rows = V_HEAD_DIM + V_ONES_ROWS
    return pl.pallas_call(
        _in_proj_kernel,
        grid=(t_tok // tm, IN_WIDTH // IN_TN),
        in_specs=[
            pl.BlockSpec((tm, D_MODEL), lambda i, j: (i, 0), pipeline_mode=pl.Buffered(1)),
            pl.BlockSpec((1, D_MODEL), lambda i, j: (0, 0)),
            pl.BlockSpec((D_MODEL, IN_TN), lambda i, j: (0, j)),
            tab_spec, tab_spec, tab_spec,
        ],
        out_specs=[
            pl.BlockSpec((tm, IN_TN), lambda i, j: (i, jnp.minimum(j, HY_TILES - 1))),
            tile(LANES), head_spec, tile(v_rows),
        ],
        out_shape=[jax.ShapeDtypeStruct((t_tok, HYENA_IN), F32), tile_shape(LANES), head_shape, tile_shape(v_rows)],
        scratch_shapes=[pltpu.VMEM((tm, D_MODEL), BF16)],
        compiler_params=_cparams(("parallel", "arbitrary")),
        name="in_proj",
    )(x, g, w, cos_t, sa_t, sb_t)


def _short_conv_kernel(x_ref, w_ref, b_ref, o_ref):
    x = x_ref[...]
    n = x.shape[0]
    row = lax.broadcasted_iota(jnp.int32, x.shape, 0)
    prev = jnp.where(row == 0, 0.0, pltpu.roll(x, 1, 0))
    nxt = jnp.where(row == n - 1, 0.0, pltpu.roll(x, n - 1, 0))
    o_ref[...] = b_ref[...] + prev * w_ref[0:1, :] + x * w_ref[1:2, :] + nxt * w_ref[2:3, :]


def _short_conv(hy, w, b):
    n_seq, seq_len, width = hy.shape
    return pl.pallas_call(
        _short_conv_kernel,
        grid=(n_seq, width // LANES),
        in_specs=[
            pl.BlockSpec((None, seq_len, LANES), lambda s, c: (s, 0, c)),
            pl.BlockSpec((SUBLANES, LANES), lambda s, c: (0, c)),
            pl.BlockSpec((1, LANES), lambda s, c: (0, c)),
        ],
        out_specs=pl.BlockSpec((None, seq_len, LANES), lambda s, c: (s, 0, c)),
        out_shape=jax.ShapeDtypeStruct(hy.shape, F32),
        compiler_params=_cparams(("parallel", "parallel")),
        name="short_conv",
    )(hy, w, b)


def _filter_kernel(z_ref, w1_ref, b1_ref, f1_ref, w2_ref, b2_ref, f2_ref,
                   w3_ref, b3_ref, f3_ref, w4_ref, delta_ref, o_ref, *, seq_len):
    hp = lax.Precision.HIGHEST
    z = z_ref[...]
    h = jnp.sin(f1_ref[...] * (jnp.dot(z, w1_ref[...], precision=hp, preferred_element_type=F32) + b1_ref[...]))
    h = jnp.sin(f2_ref[...] * (jnp.dot(h, w2_ref[...], precision=hp, preferred_element_type=F32) + b2_ref[...]))
    h = jnp.sin(f3_ref[...] * (jnp.dot(h, w3_ref[...], precision=hp, preferred_element_type=F32) + b3_ref[...]))
    rb = z.shape[0]
    n = pl.program_id(0) * rb + lax.broadcasted_iota(jnp.int32, (rb, 1), 0)
    m_fwd = (n < seq_len).astype(F32)
    m_bwd = jnp.logical_or(n == 0, n > seq_len).astype(F32)
    decay = jnp.exp(-z[:, 0:1] * delta_ref[...])
    cw = HYENA_WIDTH
    hb = h.astype(BF16)
    for order in range(2):
        base = order * 2 * cw
        fwd = jnp.dot(hb, w4_ref[:, base:base + cw], preferred_element_type=F32)
        bwd = jnp.dot(hb, w4_ref[:, base + cw:base + 2 * cw], preferred_element_type=F32)
        o_ref[order] = (m_fwd * fwd + m_bwd * bwd) * decay


def _filters(zfeat, w1, b1, f1, w2, b2, f2, w3, b3, f3, w4, delta, seq_len, rb=512):
    n2 = zfeat.shape[0]
    full = lambda a: pl.BlockSpec(a.shape, lambda i: (0,) * a.ndim)
    args = (w1, b1, f1, w2, b2, f2, w3, b3, f3, w4, delta)
    return pl.pallas_call(
        functools.partial(_filter_kernel, seq_len=seq_len),
        grid=(n2 // rb,),
        in_specs=[pl.BlockSpec((rb, zfeat.shape[1]), lambda i: (i, 0))] + [full(a) for a in args],
        out_specs=pl.BlockSpec((2, rb, HYENA_WIDTH), lambda i: (0, i, 0)),
        out_shape=jax.ShapeDtypeStruct((2, n2, HYENA_WIDTH), F32),
        compiler_params=_cparams(("parallel",)),
        name="hyena_filter",
    )(zfeat, *args)


DFT_CB = 256
DFT_KK = 8


def _dft_a_kernel(u_ref, m_ref, o_ref):
    cb = u_ref.shape[-1]
    x = u_ref[...].reshape(DFT_RADIX * SUBLANES, cb).astype(BF16)
    r = jnp.dot(m_ref[...], x, preferred_element_type=F32)
    o_ref[...] = r.reshape(o_ref.shape)


def _dft_a(u6, m_a, ch_off):
    g = u6.shape[0]
    lead = u6.shape[1:-3]
    nbc = u6.shape[-3]
    cout = HYENA_WIDTH
    blk_in = (None,) + lead + (None, SUBLANES, DFT_CB)
    zeros = (0,) * len(lead)
    return pl.pallas_call(
        _dft_a_kernel,
        grid=(nbc, g, cout // DFT_CB),
        in_specs=[
            pl.BlockSpec(blk_in, lambda bc, p, c: (p,) + zeros + (bc, 0, c + ch_off // DFT_CB)),
            pl.BlockSpec((None,) + m_a.shape[1:], lambda bc, p, c: (bc, 0, 0)),
        ],
        out_specs=pl.BlockSpec((None, 2, DFT_RADIX, None, SUBLANES, DFT_CB),
                               lambda bc, p, c: (p, 0, 0, bc, 0, c)),
        out_shape=jax.ShapeDtypeStruct((g, 2, DFT_RADIX, nbc, SUBLANES, cout), F32),
        compiler_params=_cparams(("arbitrary", "arbitrary", "arbitrary")),
        name="dft_stage_a",
    )(u6, m_a)


def _dft_b_conv_kernel(a_ref, kf_ref, fb_ref, fbi_ref, g_ref):
    r = DFT_RADIX
    for i in range(DFT_KK):
        s = jnp.concatenate([a_ref[0, i], a_ref[1, i]], axis=0).astype(BF16)
        x = jnp.dot(fb_ref[...], s, preferred_element_type=F32)
        xr, xi = x[:r], x[r:]
        kr, ki = kf_ref[0, i], kf_ref[1, i]
        y = jnp.concatenate([xr * kr - xi * ki, xr * ki + xi * kr], axis=0).astype(BF16)
        gg = jnp.dot(fbi_ref[...], y, preferred_element_type=F32)
        g_ref[0, i] = gg[:r]
        g_ref[1, i] = gg[r:]


def _dft_b_conv(a1, kf, order, fb, fbi):
    p, _, r, _, c = a1.shape
    blk = pl.BlockSpec((None, 2, DFT_KK, r, DFT_CB), lambda k, cc, pp: (pp, 0, k, 0, cc))
    mat = pl.BlockSpec((2 * r, 2 * r), lambda k, cc, pp: (0, 0))
    return pl.pallas_call(
        _dft_b_conv_kernel,
        grid=(r // DFT_KK, c // DFT_CB, p),
        in_specs=[blk, pl.BlockSpec((None, 2, DFT_KK, r, DFT_CB), lambda k, cc, pp: (order, 0, k, 0, cc)),
                  mat, mat],
        out_specs=blk,
        out_shape=jax.ShapeDtypeStruct(a1.shape, F32),
        compiler_params=_cparams(("parallel", "parallel", "arbitrary")),
        name="dft_stage_b_conv",
    )(a1, kf, fb, fbi)


def _dft_b_fwd_kernel(a_ref, fb_ref, o_ref, *, scale):
    r = DFT_RADIX
    for i in range(DFT_KK):
        s = jnp.concatenate([a_ref[0, i], a_ref[1, i]], axis=0).astype(BF16)
        x = jnp.dot(fb_ref[...], s, preferred_element_type=F32) * scale
        o_ref[0, i] = x[:r]
        o_ref[1, i] = x[r:]


def _dft_b_fwd(a1, fb, scale):
    p, _, r, _, c = a1.shape
    blk = pl.BlockSpec((None, 2, DFT_KK, r, DFT_CB), lambda k, cc, pp: (pp, 0, k, 0, cc))
    return pl.pallas_call(
        functools.partial(_dft_b_fwd_kernel, scale=scale),
        grid=(r // DFT_KK, c // DFT_CB, p),
        in_specs=[blk, pl.BlockSpec((2 * r, 2 * r), lambda k, cc, pp: (0, 0))],
        out_specs=blk,
        out_shape=jax.ShapeDtypeStruct(a1.shape, F32),
        compiler_params=_cparams(("parallel", "parallel", "arbitrary")),
        name="dft_stage_b_filter",
    )(a1, fb)


def _dft_c_gate_kernel(g_ref, m_ref, x_ref, v_ref, bias_ref, o_ref):
    cb = g_ref.shape[-1]
    g = g_ref[...].reshape(2 * DFT_RADIX * SUBLANES, cb).astype(BF16)
    y = jnp.dot(m_ref[...], g, preferred_element_type=F32).reshape(o_ref.shape)
    o_ref[...] = x_ref[...] * (y + v_ref[...] * bias_ref[...])


def _dft_c_gate(g6, m_c, xg, xg_off, v, v_off, bias):
    p, _, r, nbc, _, c = g6.shape
    half = r // 2
    tspec = lambda off: pl.BlockSpec((None, 2, half, None, SUBLANES, DFT_CB),
                                     lambda bc, pp, cc: (pp, 0, 0, bc, 0, cc + off // DFT_CB))
    return pl.pallas_call(
        _dft_c_gate_kernel,
        grid=(nbc, p, c // DFT_CB),
        in_specs=[
            pl.BlockSpec((None, 2, r, None, SUBLANES, DFT_CB), lambda bc, pp, cc: (pp, 0, 0, bc, 0, cc)),
            pl.BlockSpec((None,) + m_c.shape[1:], lambda bc, pp, cc: (bc, 0, 0)),
            tspec(xg_off), tspec(v_off),
            pl.BlockSpec((1, DFT_CB), lambda bc, pp, cc: (0, cc)),
        ],
        out_specs=tspec(0),
        out_shape=jax.ShapeDtypeStruct((p, 2, half, nbc, SUBLANES, c), F32),
        compiler_params=_cparams(("arbitrary", "arbitrary", "arbitrary")),
        name="dft_stage_c_gate",
    )(g6, m_c, xg, v, bias)


def _dft_tables(seq_len):
    r = DFT_RADIX
    n = 2 * seq_len
    nbc = r // SUBLANES
    i32 = jnp.int32
    bc = jnp.arange(nbc, dtype=i32).reshape(nbc, 1, 1, 1)
    k1 = jnp.arange(r, dtype=i32).reshape(1, r, 1, 1)
    jj = jnp.arange(SUBLANES, dtype=i32).reshape(1, 1, SUBLANES, 1)
    aa = jnp.arange(r, dtype=i32).reshape(1, 1, 1, r)
    ang = ((k1 * (r * aa + SUBLANES * bc + jj)) % n).astype(F32) * (2.0 * math.pi / n)
    c, s = jnp.cos(ang), jnp.sin(ang)
    eye = jnp.eye(SUBLANES, dtype=F32)

    def expand(t):
        return t[..., None] * eye[None, None, :, None, :]

    half = r // 2
    ch, sh = c[..., :half], s[..., :half]
    re_rows = jnp.stack([expand(ch), expand(sh)], axis=3)
    im_rows = jnp.stack([expand(-sh), expand(ch)], axis=3)
    m_pair = jnp.stack([re_rows, im_rows], axis=1).reshape(nbc, 2 * r * SUBLANES, r * SUBLANES)
    m_real = jnp.stack([expand(c), expand(-s)], axis=1).reshape(nbc, 2 * r * SUBLANES, r * SUBLANES)
    m_pair = m_pair.astype(BF16)
    m_inv = jnp.swapaxes(m_pair, 1, 2)
    kb = jnp.arange(r, dtype=i32)
    angb = ((kb[:, None] * kb[None, :]) % r).astype(F32) * (2.0 * math.pi / r)
    cb_, sb_ = jnp.cos(angb), jnp.sin(angb)
    fb = jnp.block([[cb_, sb_], [-sb_, cb_]]).astype(BF16)
    fbi = jnp.block([[cb_, -sb_], [sb_, cb_]]).astype(BF16)
    return m_pair, m_real.astype(BF16), m_inv, fb, fbi


def _filter_features(seq_len):
    n2 = 2 * seq_len
    n = jnp.arange(n2)
    pos = jnp.where(n <= seq_len, n, n2 - n)
    pos = jnp.minimum(pos, seq_len - 1)
    t_lin = jnp.linspace(0.0, 1.0, seq_len, dtype=F32)
    omega = 2.0 * math.pi * jnp.arange(seq_len, dtype=F32) / seq_len
    bands = jnp.linspace(1e-4, FILTER_BANDS - 1, FILTER_BANDS, dtype=F32)
    phase = omega[:, None] * bands[None, :]
    z = jnp.concatenate([t_lin[:, None], jnp.cos(phase), -jnp.sin(phase)], axis=-1)
    z = jnp.pad(z, ((0, 0), (0, FILTER_HIDDEN - FILTER_EMB)))
    return z[pos]


ATTN_Q_TILES = 2


def _attn_kernel(lq1_ref, lk1_ref, lq2_ref, lk2_ref, subln_ref, qt_ref, k_ref, vt_ref, o_ref,
                 acc_scr, s_scr, *, lam_init):
    qt = jnp.concatenate([qt_ref[t] for t in range(qt_ref.shape[0])], axis=1)
    tq = qt.shape[1]
    tc = vt_ref.shape[2]
    row = lax.broadcasted_iota(jnp.int32, qt.shape, 0)
    zero = jnp.zeros_like(qt)
    qm = (jnp.where(row < HEAD_DIM, qt, zero), jnp.where(row >= HEAD_DIM, qt, zero))
    acc_scr[...] = jnp.zeros(acc_scr.shape, F32)
    n_chunks = vt_ref.shape[0]

    def scores(c, slot):
        k_c = k_ref[pl.ds(pl.multiple_of(c * tc, tc), tc), :]
        cms = []
        for i in range(2):
            st = jnp.dot(k_c, qm[i], preferred_element_type=F32)
            s_scr[slot, i] = st
            cms.append(jnp.max(st, axis=0, keepdims=True))
        return tuple(cms)

    def softmax_pv(c, slot, cms, ms):
        vt_c = vt_ref[c]
        new_m = []
        for i in range(2):
            m_new = jnp.maximum(ms[i], cms[i])
            alpha = jnp.exp2(ms[i] - m_new)
            p = jnp.exp2(s_scr[slot, i] - m_new)
            acc_scr[i] = alpha * acc_scr[i] + jnp.dot(vt_c, p.astype(BF16), preferred_element_type=F32)
            new_m.append(m_new)
        return tuple(new_m)

    def pair(j, carry):
        cm, ms = carry
        c = 2 * j
        cm1 = scores(c + 1, 1)
        ms = softmax_pv(c, 0, cm, ms)
        cm2 = scores(c + 2, 0)
        ms = softmax_pv(c + 1, 1, cm1, ms)
        return cm2, ms

    neg = jnp.full((1, tq), -jnp.inf, F32)
    cm, ms = lax.fori_loop(0, n_chunks // 2 - 1, pair, (scores(0, 0), (neg, neg)))
    cm1 = scores(n_chunks - 1, 1)
    ms = softmax_pv(n_chunks - 2, 0, cm, ms)
    ms = softmax_pv(n_chunks - 1, 1, cm1, ms)

    lam = (jnp.exp(jnp.sum(lq1_ref[...] * lk1_ref[...], axis=1, keepdims=True))
           - jnp.exp(jnp.sum(lq2_ref[...] * lk2_ref[...], axis=1, keepdims=True)) + lam_init)
    vd = V_HEAD_DIM
    ot = (acc_scr[0, :vd] * (1.0 / acc_scr[0, vd:vd + 1])
          - lam * (acc_scr[1, :vd] * (1.0 / acc_scr[1, vd:vd + 1])))
    o = _rms(ot.T, subln_ref[...]) * (1.0 - lam_init)
    o_ref[...] = o.astype(o_ref.dtype)


def _attention(qt, k, vt, lq1, lk1, lq2, lk2, subln, lam_init):
    n_seq, n_heads, n_tiles, _, tc = qt.shape
    seq_len = k.shape[2]
    tq = ATTN_Q_TILES * tc
    v_rows = vt.shape[3]
    vec = lambda a: pl.BlockSpec(a.shape, lambda s, h, i: (0, 0))
    return pl.pallas_call(
        functools.partial(_attn_kernel, lam_init=lam_init),
        grid=(n_seq, n_heads, seq_len // tq),
        in_specs=[
            vec(lq1), vec(lk1), vec(lq2), vec(lk2), vec(subln),
            pl.BlockSpec((None, None, ATTN_Q_TILES, LANES, tc), lambda s, h, i: (s, h, i, 0, 0)),
            pl.BlockSpec((None, None, seq_len, LANES), lambda s, h, i: (s, h, 0, 0)),
            pl.BlockSpec((None, None, n_tiles, v_rows, tc), lambda s, h, i: (s, h, 0, 0, 0)),
        ],
        out_specs=pl.BlockSpec((None, tq, LANES), lambda s, h, i: (s, i, h)),
        out_shape=jax.ShapeDtypeStruct((n_seq, seq_len, n_heads * V_HEAD_DIM), BF16),
        scratch_shapes=[pltpu.VMEM((2, v_rows, tq), F32), pltpu.VMEM((2, 2, tc, tq), F32)],
        compiler_params=_cparams(("parallel", "parallel", "arbitrary")),
        name="diff_attention",
    )(lq1, lk1, lq2, lk2, subln, qt, k, vt)


def _out_proj_kernel(z_ref, a_ref, gh_ref, wt_ref, wb_ref, gp_ref, x_ref, o_ref):
    hn = _rms(z_ref[...], gh_ref[...]).astype(BF16)
    mix = (jnp.dot(hn, wt_ref[...], preferred_element_type=F32)
           + jnp.dot(a_ref[...], wb_ref[...], preferred_element_type=F32))
    o_ref[...] = x_ref[...] + _rms(mix, gp_ref[...])


def _out_proj(z, at, g_hy, w_out, g_post, x, tm=512):
    t_tok = x.shape[0]
    row = lambda w: pl.BlockSpec((tm, w), lambda i: (i, 0))
    return pl.pallas_call(
        _out_proj_kernel,
        grid=(t_tok // tm,),
        in_specs=[
            row(HYENA_WIDTH), row(ATTN_WIDTH),
            pl.BlockSpec((1, HYENA_WIDTH), lambda i: (0, 0)),
            pl.BlockSpec((HYENA_WIDTH, D_MODEL), lambda i: (0, 0)),
            pl.BlockSpec((ATTN_WIDTH, D_MODEL), lambda i: (1, 0)),
            pl.BlockSpec((1, D_MODEL), lambda i: (0, 0)),
            row(D_MODEL),
        ],
        out_specs=row(D_MODEL),
        out_shape=jax.ShapeDtypeStruct(x.shape, F32),
        compiler_params=_cparams(("parallel",)),
        name="out_proj",
    )(z, at, g_hy, w_out, w_out, g_post, x)


FFN_TF = 512


def _ffn_kernel(x_ref, gpre_ref, wg_ref, wu_ref, wd_ref, gpost_ref, o_ref, h_scr, acc_scr):
    f = pl.program_id(1)

    @pl.when(f == 0)
    def _():
        h_scr[...] = _rms(x_ref[...], gpre_ref[...]).astype(BF16)
        acc_scr[...] = jnp.zeros(acc_scr.shape, F32)

    h = h_scr[...]
    gate = jnp.dot(h, wg_ref[...], preferred_element_type=F32)
    up = jnp.dot(h, wu_ref[...], preferred_element_type=F32)
    act = (gate * jax.nn.sigmoid(gate) * up).astype(BF16)
    acc_scr[...] += jnp.dot(act, wd_ref[...], preferred_element_type=F32)

    @pl.when(f == pl.num_programs(1) - 1)
    def _():
        o_ref[...] = x_ref[...] + _rms(acc_scr[...], gpost_ref[...])


def _ffn(x, g_pre, w_gate, w_up, w_down, g_post, tm=1024):
    t_tok = x.shape[0]
    return pl.pallas_call(
        _ffn_kernel,
        grid=(t_tok // tm, D_FF // FFN_TF),
        in_specs=[
            pl.BlockSpec((tm, D_MODEL), lambda i, f: (i, 0), pipeline_mode=pl.Buffered(1)),
            pl.BlockSpec((1, D_MODEL), lambda i, f: (0, 0)),
            pl.BlockSpec((D_MODEL, FFN_TF), lambda i, f: (0, f)),
            pl.BlockSpec((D_MODEL, FFN_TF), lambda i, f: (0, f)),
            pl.BlockSpec((FFN_TF, D_MODEL), lambda i, f: (f, 0)),
            pl.BlockSpec((1, D_MODEL), lambda i, f: (0, 0)),
        ],
        out_specs=pl.BlockSpec((tm, D_MODEL), lambda i, f: (i, 0), pipeline_mode=pl.Buffered(1)),
        out_shape=jax.ShapeDtypeStruct(x.shape, F32),
        scratch_shapes=[pltpu.VMEM((tm, D_MODEL), BF16), pltpu.VMEM((tm, D_MODEL), F32)],
        compiler_params=_cparams(("parallel", "arbitrary")),
        name="ffn",
    )(x, g_pre, w_gate, w_up, w_down, g_post)


def _rope_tables(seq_len):
    inv_freq = 1.0 / (ROPE_THETA ** (jnp.arange(0, HEAD_DIM, 2, dtype=F32) / HEAD_DIM))
    ang = jnp.arange(seq_len, dtype=F32)[:, None] * inv_freq[None, :]
    cos, sin = jnp.cos(ang), jnp.sin(ang)
    zero = jnp.zeros_like(sin)
    reps = LANES // HEAD_DIM
    cos_t = jnp.tile(jnp.concatenate([cos, cos], axis=1), (1, reps))
    sa_t = jnp.tile(jnp.concatenate([-sin, zero], axis=1), (1, reps))
    sb_t = jnp.tile(jnp.concatenate([zero, sin], axis=1), (1, reps))
    return cos_t, sa_t, sb_t


def _hyena_conv(u_v, v_off, xg, xg_off, kf, order, bias, tabs):
    m_pair, _, m_inv, fb, fbi = tabs
    a1 = _dft_a(u_v, m_pair, v_off)
    p, _, r, nbc, sl, c = a1.shape
    g = _dft_b_conv(a1.reshape(p, 2, r, nbc * sl, c), kf, order, fb, fbi)
    return _dft_c_gate(g.reshape(a1.shape), m_inv, xg, xg_off, u_v, v_off, bias)


def _filter_spectra(li, seq_len, tabs, zfeat, delta, w):
    (filt_w1, filt_b1, filt_f1, filt_w2, filt_b2, filt_f2, filt_w3, filt_b3, filt_f3, filt_w4) = w[7:17]
    row = lambda a: a[li].reshape(1, -1)
    r = DFT_RADIX
    nbc = r // SUBLANES
    pad_w1 = jnp.pad(filt_w1[li], ((0, FILTER_HIDDEN - FILTER_EMB), (0, 0)))
    kt = _filters(zfeat, pad_w1, row(filt_b1), row(filt_f1), filt_w2[li], row(filt_b2), row(filt_f2),
                  filt_w3[li], row(filt_b3), row(filt_f3), filt_w4[li].astype(BF16), delta, seq_len)
    ka = _dft_a(kt.reshape(2, r, nbc, SUBLANES, HYENA_WIDTH), tabs[1], 0)
    return _dft_b_fwd(ka.reshape(2, 2, r, r, HYENA_WIDTH), tabs[3], 1.0 / (2 * seq_len))


def _layer(x, li, n_seq, seq_len, rope, tabs, kf, w):
    (norm_mix_pre, norm_mix_post, norm_ffn_pre, norm_ffn_post, w_in, conv_w, conv_b,
     _, _, _, _, _, _, _, _, _, _,
     hyena_bias, hyena_norm, lam_q1, lam_k1, lam_q2, lam_k2, subln,
     w_out, w_gate, w_up, w_down) = w
    lam_init = 0.8 - 0.6 * math.exp(-0.3 * li)
    row = lambda a: a[li].reshape(1, -1)
    r = DFT_RADIX
    nbc = r // SUBLANES
    n_pair = n_seq // 2

    hy, q, k, v = _in_proj(x, row(norm_mix_pre), w_in[li], *rope, n_seq, seq_len)

    cw = jnp.pad(conv_w[li], ((0, SUBLANES - conv_w.shape[1]), (0, 0)))
    u = _short_conv(hy.reshape(n_seq, seq_len, HYENA_IN), cw, row(conv_b))
    u6 = u.reshape(n_pair, 2, r // 2, nbc, SUBLANES, HYENA_IN)
    z1 = _hyena_conv(u6, 2 * HYENA_WIDTH, u6, 0, kf, 0, hyena_bias[li, 0].reshape(1, -1), tabs)
    z2 = _hyena_conv(z1, 0, u6, HYENA_WIDTH, kf, 1, hyena_bias[li, 1].reshape(1, -1), tabs)

    at = _attention(q, k, v, row(lam_q1), row(lam_k1), row(lam_q2), row(lam_k2), row(subln), lam_init)

    x = _out_proj(z2.reshape(n_seq * seq_len, HYENA_WIDTH), at.reshape(n_seq * seq_len, ATTN_WIDTH),
                  row(hyena_norm), w_out[li], row(norm_mix_post), x)
    return _ffn(x, row(norm_ffn_pre), w_gate[li], w_up[li], w_down[li], row(norm_ffn_post))


def kernel(x_prompt, x_sample, norm_mix_pre, norm_mix_post, norm_ffn_pre, norm_ffn_post, w_in, conv_w, conv_b, filt_w1, filt_b1, filt_f1, filt_w2, filt_b2, filt_f2, filt_w3, filt_b3, filt_f3, filt_w4, hyena_bias, hyena_norm, lam_q1, lam_k1, lam_q2, lam_k2, subln, w_out, w_gate, w_up, w_down):
    assert x_prompt.shape[1:] == x_sample.shape[1:]
    n_p, seq_len, d = x_prompt.shape
    n_s = x_sample.shape[0]
    assert n_p % 2 == 0 and n_s % 2 == 0 and seq_len == DFT_RADIX * DFT_RADIX // 2 and d == D_MODEL
    xs = [x_prompt.reshape(n_p * seq_len, d), x_sample.reshape(n_s * seq_len, d)]
    w = (norm_mix_pre, norm_mix_post, norm_ffn_pre, norm_ffn_post, w_in.astype(BF16), conv_w, conv_b,
         filt_w1, filt_b1, filt_f1, filt_w2, filt_b2, filt_f2, filt_w3, filt_b3, filt_f3, filt_w4,
         hyena_bias, hyena_norm, lam_q1, lam_k1, lam_q2, lam_k2, subln,
         w_out.astype(BF16), w_gate.astype(BF16), w_up.astype(BF16), w_down.astype(BF16))
    rope = _rope_tables(seq_len)
    tabs = _dft_tables(seq_len)
    zfeat = _filter_features(seq_len)
    delta = jnp.abs(jnp.linspace(MIN_DECAY, MAX_DECAY, HYENA_WIDTH, dtype=F32)).reshape(1, -1)
    for li in range(DEPTH):
        kf = _filter_spectra(li, seq_len, tabs, zfeat, delta, w)
        xs = [_layer(x, li, x.shape[0] // seq_len, seq_len, rope, tabs, kf, w) for x in xs]
    return (xs[0].reshape(n_p, seq_len, d), xs[1].reshape(n_s, seq_len, d))
```

```python
import functools
import math

import jax
import jax.numpy as jnp
from jax import lax
from jax.experimental import pallas as pl
from jax.experimental.pallas import tpu as pltpu

F32 = jnp.float32
BF16 = jnp.bfloat16

D_MODEL = 2048
DEPTH = 4
HYENA_WIDTH = 1024
ATTN_WIDTH = 1024
N_HEADS = 8
HEAD_DIM = 64
V_HEAD_DIM = 128
HYENA_IN = 3 * HYENA_WIDTH
IN_WIDTH = HYENA_IN + 3 * ATTN_WIDTH
FILTER_EMB = 33
FILTER_BANDS = 16
FILTER_HIDDEN = 64
DECAY_TARGET = 1e-2
MIN_DECAY = math.log(DECAY_TARGET) / 0.3
MAX_DECAY = math.log(DECAY_TARGET) / 1.5
D_FF = 5632
ROPE_THETA = 10000.0
EPS = 1e-6
LOG2E = math.log2(math.e)

LANES = 128
SUBLANES = 8
DFT_RADIX = 128
VMEM_LIMIT = 56 * 1024 * 1024


def _cparams(sem):
    return pltpu.CompilerParams(dimension_semantics=sem, vmem_limit_bytes=VMEM_LIMIT)


def _rms(xf, g):
    return xf * lax.rsqrt(jnp.mean(xf * xf, axis=-1, keepdims=True) + EPS) * g


IN_TN = 1024
HY_TILES = HYENA_IN // IN_TN


def _in_proj_kernel(x_ref, g_ref, w_ref, cos_ref, sa_ref, sb_ref,
                    hy_ref, q_ref, k_ref, v_ref, h_scr):
    j = pl.program_id(1)

    @pl.when(j == 0)
    def _():
        h_scr[...] = _rms(x_ref[...], g_ref[...]).astype(BF16)

    res = jnp.dot(h_scr[...], w_ref[...], preferred_element_type=F32)

    @pl.when(j < HY_TILES)
    def _():
        hy_ref[...] = res

    def rope(sl):
        return (sl * cos_ref[...] + pltpu.roll(sl, LANES - HEAD_DIM // 2, 1) * sa_ref[...]
                + pltpu.roll(sl, HEAD_DIM // 2, 1) * sb_ref[...])

    n_tiles = q_ref.shape[1]
    tt = q_ref.shape[3]

    @pl.when(j == HY_TILES)
    def _():
        for h in range(N_HEADS):
            ro = rope(res[:, h * LANES:(h + 1) * LANES]) * (HEAD_DIM ** -0.5 * LOG2E)
            for t in range(n_tiles):
                q_ref[h, t] = ro[t * tt:(t + 1) * tt].T.astype(BF16)

    @pl.when(j == HY_TILES + 1)
    def _():
        for h in range(N_HEADS):
            k_ref[h] = rope(res[:, h * LANES:(h + 1) * LANES]).astype(BF16)

    @pl.when(j == HY_TILES + 2)
    def _():
        ones = jnp.ones((V_ONES_ROWS, tt), BF16)
        for h in range(N_HEADS):
            for t in range(n_tiles):
                v_ref[h, t, :V_HEAD_DIM] = res[t * tt:(t + 1) * tt, h * LANES:(h + 1) * LANES].T.astype(BF16)
                v_ref[h, t, V_HEAD_DIM:] = ones


ATTN_TILE = 512
V_ONES_ROWS = 16


def _in_proj(x, g, w, cos_t, sa_t, sb_t, n_seq, seq_len, tm=512):
    t_tok = x.shape[0]
    lt = seq_len // tm
    tt = ATTN_TILE
    head_spec = pl.BlockSpec((None, N_HEADS, tm, LANES), lambda i, j: (i // lt, 0, i % lt, 0))
    tile = lambda rows: pl.BlockSpec((None, N_HEADS, tm // tt, rows, tt), lambda i, j: (i // lt, 0, i % lt, 0, 0))
    tab_spec = pl.BlockSpec((tm, LANES), lambda i, j: (i % lt, 0))
    head_shape = jax.ShapeDtypeStruct((n_seq, N_HEADS, seq_len, LANES), BF16)
    tile_shape = lambda rows: jax.ShapeDtypeStruct((n_seq, N_HEADS, seq_len // tt, rows, tt), BF16)
    v_rows = V_HEAD_DIM + V_ONES_ROWS
    return pl.pallas_call(
        _in_proj_kernel,
        grid=(t_tok // tm, IN_WIDTH // IN_TN),
        in_specs=[
            pl.BlockSpec((tm, D_MODEL), lambda i, j: (i, 0)),
            pl.BlockSpec((1, D_MODEL), lambda i, j: (0, 0)),
            pl.BlockSpec((D_MODEL, IN_TN), lambda i, j: (0, j)),
            tab_spec, tab_spec, tab_spec,
        ],
        out_specs=[
            pl.BlockSpec((tm, IN_TN), lambda i, j: (i, jnp.minimum(j, HY_TILES - 1))),
            tile(LANES), head_spec, tile(v_rows),
        ],
        out_shape=[jax.ShapeDtypeStruct((t_tok, HYENA_IN), F32), tile_shape(LANES), head_shape, tile_shape(v_rows)],
        scratch_shapes=[pltpu.VMEM((tm, D_MODEL), BF16)],
        compiler_params=_cparams(("parallel", "arbitrary")),
        name="in_proj",
    )(x, g, w, cos_t, sa_t, sb_t)


def _short_conv_kernel(x_ref, w_ref, b_ref, o_ref):
    x = x_ref[...]
    n = x.shape[0]
    row = lax.broadcasted_iota(jnp.int32, x.shape, 0)
    prev = jnp.where(row == 0, 0.0, pltpu.roll(x, 1, 0))
    nxt = jnp.where(row == n - 1, 0.0, pltpu.roll(x, n - 1, 0))
    o_ref[...] = b_ref[...] + prev * w_ref[0:1, :] + x * w_ref[1:2, :] + nxt * w_ref[2:3, :]


def _short_conv(hy, w, b):
    n_seq, seq_len, width = hy.shape
    return pl.pallas_call(
        _short_conv_kernel,
        grid=(n_seq, width // LANES),
        in_specs=[
            pl.BlockSpec((None, seq_len, LANES), lambda s, c: (s, 0, c)),
            pl.BlockSpec((SUBLANES, LANES), lambda s, c: (0, c)),
            pl.BlockSpec((1, LANES), lambda s, c: (0, c)),
        ],
        out_specs=pl.BlockSpec((None, seq_len, LANES), lambda s, c: (s, 0, c)),
        out_shape=jax.ShapeDtypeStruct(hy.shape, F32),
        compiler_params=_cparams(("parallel", "parallel")),
        name="short_conv",
    )(hy, w, b)


def _filter_kernel(z_ref, w1_ref, b1_ref, f1_ref, w2_ref, b2_ref, f2_ref,
                   w3_ref, b3_ref, f3_ref, w4_ref, delta_ref, o_ref, *, seq_len):
    hp = lax.Precision.HIGHEST
    z = z_ref[...]
    h = jnp.sin(f1_ref[...] * (jnp.dot(z, w1_ref[...], precision=hp, preferred_element_type=F32) + b1_ref[...]))
    h = jnp.sin(f2_ref[...] * (jnp.dot(h, w2_ref[...], precision=hp, preferred_element_type=F32) + b2_ref[...]))
    h = jnp.sin(f3_ref[...] * (jnp.dot(h, w3_ref[...], precision=hp, preferred_element_type=F32) + b3_ref[...]))
    rb = z.shape[0]
    n = pl.program_id(0) * rb + lax.broadcasted_iota(jnp.int32, (rb, 1), 0)
    m_fwd = (n < seq_len).astype(F32)
    m_bwd = jnp.logical_or(n == 0, n > seq_len).astype(F32)
    decay = jnp.exp(-z[:, 0:1] * delta_ref[...])
    cw = HYENA_WIDTH
    hb = h.astype(BF16)
    for order in range(2):
        base = order * 2 * cw
        fwd = jnp.dot(hb, w4_ref[:, base:base + cw], preferred_element_type=F32)
        bwd = jnp.dot(hb, w4_ref[:, base + cw:base + 2 * cw], preferred_element_type=F32)
        o_ref[order] = (m_fwd * fwd + m_bwd * bwd) * decay


def _filters(zfeat, w1, b1, f1, w2, b2, f2, w3, b3, f3, w4, delta, seq_len, rb=512):
    n2 = zfeat.shape[0]
    full = lambda a: pl.BlockSpec(a.shape, lambda i: (0,) * a.ndim)
    args = (w1, b1, f1, w2, b2, f2, w3, b3, f3, w4, delta)
    return pl.pallas_call(
        functools.partial(_filter_kernel, seq_len=seq_len),
        grid=(n2 // rb,),
        in_specs=[pl.BlockSpec((rb, zfeat.shape[1]), lambda i: (i, 0))] + [full(a) for a in args],
        out_specs=pl.BlockSpec((2, rb, HYENA_WIDTH), lambda i: (0, i, 0)),
        out_shape=jax.ShapeDtypeStruct((2, n2, HYENA_WIDTH), F32),
        compiler_params=_cparams(("parallel",)),
        name="hyena_filter",
    )(zfeat, *args)


DFT_CB = 256
DFT_KK = 8


U32 = jnp.uint32


def _pack_complex(re, im):
    def rounded(x):
        b = pltpu.bitcast(x, U32)
        return b + U32(0x7FFF) + ((b >> 16) & U32(1))
    return (rounded(re) & U32(0xFFFF0000)) | (rounded(im) >> 16)


def _unpack_complex(w):
    return pltpu.bitcast(w & U32(0xFFFF0000), F32), pltpu.bitcast(w << 16, F32)


def _dft_a_kernel(u_ref, m_ref, o_ref):
    cb = u_ref.shape[-1]
    x = u_ref[...].reshape(DFT_RADIX * SUBLANES, cb).astype(BF16)
    r = jnp.dot(m_ref[...], x, preferred_element_type=F32)
    half = DFT_RADIX * SUBLANES
    o_ref[...] = _pack_complex(r[:half], r[half:]).reshape(o_ref.shape)


def _dft_a(u6, m_a, ch_off):
    g = u6.shape[0]
    lead = u6.shape[1:-3]
    nbc = u6.shape[-3]
    cout = HYENA_WIDTH
    blk_in = (None,) + lead + (None, SUBLANES, DFT_CB)
    zeros = (0,) * len(lead)
    return pl.pallas_call(
        _dft_a_kernel,
        grid=(nbc, g, cout // DFT_CB),
        in_specs=[
            pl.BlockSpec(blk_in, lambda bc, p, c: (p,) + zeros + (bc, 0, c + ch_off // DFT_CB)),
            pl.BlockSpec((None,) + m_a.shape[1:], lambda bc, p, c: (bc, 0, 0)),
        ],
        out_specs=pl.BlockSpec((None, DFT_RADIX, None, SUBLANES, DFT_CB),
                               lambda bc, p, c: (p, 0, bc, 0, c)),
        out_shape=jax.ShapeDtypeStruct((g, DFT_RADIX, nbc, SUBLANES, cout), U32),
        compiler_params=_cparams(("arbitrary", "arbitrary", "arbitrary")),
        name="dft_stage_a",
    )(u6, m_a)


def _dft_b_conv_kernel(a_ref, kf_ref, fb_ref, fbi_ref, g_ref):
    r = DFT_RADIX
    for i in range(DFT_KK):
        s = jnp.concatenate(_unpack_complex(a_ref[i]), axis=0).astype(BF16)
        x = jnp.dot(fb_ref[...], s, preferred_element_type=F32)
        xr, xi = x[:r], x[r:]
        kr, ki = kf_ref[0, i], kf_ref[1, i]
        y = jnp.concatenate([xr * kr - xi * ki, xr * ki + xi * kr], axis=0).astype(BF16)
        gg = jnp.dot(fbi_ref[...], y, preferred_element_type=F32)
        g_ref[i] = _pack_complex(gg[:r], gg[r:])


def _dft_b_conv(a1, kf, order, fb, fbi):
    p, r, _, c = a1.shape
    blk = pl.BlockSpec((None, DFT_KK, r, DFT_CB), lambda k, cc, pp: (pp, k, 0, cc))
    mat = pl.BlockSpec((2 * r, 2 * r), lambda k, cc, pp: (0, 0))
    return pl.pallas_call(
        _dft_b_conv_kernel,
        grid=(r // DFT_KK, c // DFT_CB, p),
        in_specs=[blk, pl.BlockSpec((None, 2, DFT_KK, r, DFT_CB), lambda k, cc, pp: (order, 0, k, 0, cc)),
                  mat, mat],
        out_specs=blk,
        out_shape=jax.ShapeDtypeStruct(a1.shape, U32),
        compiler_params=_cparams(("parallel", "parallel", "arbitrary")),
        name="dft_stage_b_conv",
    )(a1, kf, fb, fbi)


def _dft_b_fwd_kernel(a_ref, fb_ref, o_ref, *, scale):
    r = DFT_RADIX
    for i in range(DFT_KK):
        s = jnp.concatenate(_unpack_complex(a_ref[i]), axis=0).astype(BF16)
        x = jnp.dot(fb_ref[...], s, preferred_element_type=F32) * scale
        o_ref[0, i] = x[:r]
        o_ref[1, i] = x[r:]


def _dft_b_fwd(a1, fb, scale):
    p, r, _, c = a1.shape
    return pl.pallas_call(
        functools.partial(_dft_b_fwd_kernel, scale=scale),
        grid=(r // DFT_KK, c // DFT_CB, p),
        in_specs=[pl.BlockSpec((None, DFT_KK, r, DFT_CB), lambda k, cc, pp: (pp, k, 0, cc)),
                  pl.BlockSpec((2 * r, 2 * r), lambda k, cc, pp: (0, 0))],
        out_specs=pl.BlockSpec((None, 2, DFT_KK, r, DFT_CB), lambda k, cc, pp: (pp, 0, k, 0, cc)),
        out_shape=jax.ShapeDtypeStruct((p, 2, r, r, c), F32),
        compiler_params=_cparams(("parallel", "parallel", "arbitrary")),
        name="dft_stage_b_filter",
    )(a1, fb)


def _dft_c_gate_kernel(g_ref, m_ref, x_ref, v_ref, bias_ref, o_ref):
    cb = g_ref.shape[-1]
    gr, gi = _unpack_complex(g_ref[...].reshape(DFT_RADIX * SUBLANES, cb))
    g = jnp.concatenate([gr, gi], axis=0).astype(BF16)
    y = jnp.dot(m_ref[...], g, preferred_element_type=F32).reshape(o_ref.shape)
    o_ref[...] = x_ref[...] * (y + v_ref[...] * bias_ref[...])


def _dft_c_gate(g5, m_c, xg, xg_off, v, v_off, bias):
    p, r, nbc, _, c = g5.shape
    half = r // 2
    tspec = lambda off: pl.BlockSpec((None, 2, half, None, SUBLANES, DFT_CB),
                                     lambda bc, pp, cc: (pp, 0, 0, bc, 0, cc + off // DFT_CB))
    return pl.pallas_call(
        _dft_c_gate_kernel,
        grid=(nbc, p, c // DFT_CB),
        in_specs=[
            pl.BlockSpec((None, r, None, SUBLANES, DFT_CB), lambda bc, pp, cc: (pp, 0, bc, 0, cc)),
            pl.BlockSpec((None,) + m_c.shape[1:], lambda bc, pp, cc: (bc, 0, 0)),
            tspec(xg_off), tspec(v_off),
            pl.BlockSpec((1, DFT_CB), lambda bc, pp, cc: (0, cc)),
        ],
        out_specs=tspec(0),
        out_shape=jax.ShapeDtypeStruct((p, 2, half, nbc, SUBLANES, c), F32),
        compiler_params=_cparams(("arbitrary", "arbitrary", "arbitrary")),
        name="dft_stage_c_gate",
    )(g5, m_c, xg, v, bias)


def _dft_tables(seq_len):
    r = DFT_RADIX
    n = 2 * seq_len
    nbc = r // SUBLANES
    i32 = jnp.int32
    n_freq = 2 * r * SUBLANES
    n_time = r * SUBLANES

    def table(freq_axis, pair):
        shape = (nbc, n_freq, n_time) if freq_axis == 1 else (nbc, n_time, n_freq)
        bc = lax.broadcasted_iota(i32, shape, 0)
        fi = lax.broadcasted_iota(i32, shape, freq_axis)
        ti = lax.broadcasted_iota(i32, shape, 3 - freq_axis)
        part, k1, j = fi // (r * SUBLANES), (fi // SUBLANES) % r, fi % SUBLANES
        jt = ti % SUBLANES
        if pair:
            seq, a = ti // (n_time // 2), (ti // SUBLANES) % (r // 2)
        else:
            seq, a = jnp.zeros_like(ti), ti // SUBLANES
        ang = ((k1 * (r * a + SUBLANES * bc + j)) % n).astype(F32) * (2.0 * math.pi / n)
        c, s = jnp.cos(ang), jnp.sin(ang)
        val = jnp.where(part == seq, c, jnp.where(part == 0, s, -s))
        return jnp.where(j == jt, val, 0.0).astype(BF16)

    m_pair = table(1, True)
    m_real = table(1, False)
    m_inv = table(2, True)
    kb = jnp.arange(r, dtype=i32)
    angb = ((kb[:, None] * kb[None, :]) % r).astype(F32) * (2.0 * math.pi / r)
    cb_, sb_ = jnp.cos(angb), jnp.sin(angb)
    fb = jnp.block([[cb_, sb_], [-sb_, cb_]]).astype(BF16)
    fbi = jnp.block([[cb_, -sb_], [sb_, cb_]]).astype(BF16)
    return m_pair, m_real, m_inv, fb, fbi


def _filter_features(seq_len):
    n2 = 2 * seq_len
    n = jnp.arange(n2)
    pos = jnp.where(n <= seq_len, n, n2 - n)
    pos = jnp.minimum(pos, seq_len - 1)
    t_lin = jnp.linspace(0.0, 1.0, seq_len, dtype=F32)
    omega = 2.0 * math.pi * jnp.arange(seq_len, dtype=F32) / seq_len
    bands = jnp.linspace(1e-4, FILTER_BANDS - 1, FILTER_BANDS, dtype=F32)
    phase = omega[:, None] * bands[None, :]
    z = jnp.concatenate([t_lin[:, None], jnp.cos(phase), -jnp.sin(phase)], axis=-1)
    z = jnp.pad(z, ((0, 0), (0, FILTER_HIDDEN - FILTER_EMB)))
    return z[pos]


ATTN_Q_TILES = 2


def _attn_kernel(lq1_ref, lk1_ref, lq2_ref, lk2_ref, subln_ref, qt_ref, k_ref, vt_ref, o_ref,
                 acc_scr, s_scr, *, lam_init):
    qt = jnp.concatenate([qt_ref[t] for t in range(qt_ref.shape[0])], axis=1)
    tq = qt.shape[1]
    tc = vt_ref.shape[2]
    row = lax.broadcasted_iota(jnp.int32, qt.shape, 0)
    zero = jnp.zeros_like(qt)
    qm = (jnp.where(row < HEAD_DIM, qt, zero), jnp.where(row >= HEAD_DIM, qt, zero))
    acc_scr[...] = jnp.zeros(acc_scr.shape, F32)
    n_chunks = vt_ref.shape[0]

    def scores(c, slot):
        k_c = k_ref[pl.ds(pl.multiple_of(c * tc, tc), tc), :]
        cms = []
        for i in range(2):
            st = jnp.dot(k_c, qm[i], preferred_element_type=F32)
            s_scr[slot, i] = st
            cms.append(jnp.max(st, axis=0, keepdims=True))
        return tuple(cms)

    def softmax_pv(c, slot, cms, ms):
        vt_c = vt_ref[c]
        new_m = []
        for i in range(2):
            m_new = jnp.maximum(ms[i], cms[i])
            alpha = jnp.exp2(ms[i] - m_new)
            p = jnp.exp2(s_scr[slot, i] - m_new)
            acc_scr[i] = alpha * acc_scr[i] + jnp.dot(vt_c, p.astype(BF16), preferred_element_type=F32)
            new_m.append(m_new)
        return tuple(new_m)

    def pair(j, carry):
        cm, ms = carry
        c = 2 * j
        cm1 = scores(c + 1, 1)
        ms = softmax_pv(c, 0, cm, ms)
        cm2 = scores(c + 2, 0)
        ms = softmax_pv(c + 1, 1, cm1, ms)
        return cm2, ms

    neg = jnp.full((1, tq), -jnp.inf, F32)
    cm, ms = lax.fori_loop(0, n_chunks // 2 - 1, pair, (scores(0, 0), (neg, neg)), unroll=True)
    cm1 = scores(n_chunks - 1, 1)
    ms = softmax_pv(n_chunks - 2, 0, cm, ms)
    ms = softmax_pv(n_chunks - 1, 1, cm1, ms)

    lam = (jnp.exp(jnp.sum(lq1_ref[...] * lk1_ref[...], axis=1, keepdims=True))
           - jnp.exp(jnp.sum(lq2_ref[...] * lk2_ref[...], axis=1, keepdims=True)) + lam_init)
    vd = V_HEAD_DIM
    ot = (acc_scr[0, :vd] * (1.0 / acc_scr[0, vd:vd + 1])
          - lam * (acc_scr[1, :vd] * (1.0 / acc_scr[1, vd:vd + 1])))
    o = _rms(ot.T, subln_ref[...]) * (1.0 - lam_init)
    o_ref[...] = o.astype(o_ref.dtype)


def _attention(qt, k, vt, lq1, lk1, lq2, lk2, subln, lam_init):
    n_seq, n_heads, n_tiles, _, tc = qt.shape
    seq_len = k.shape[2]
    tq = ATTN_Q_TILES * tc
    v_rows = vt.shape[3]
    vec = lambda a: pl.BlockSpec(a.shape, lambda s, h, i: (0, 0))
    return pl.pallas_call(
        functools.partial(_attn_kernel, lam_init=lam_init),
        grid=(n_seq, n_heads, seq_len // tq),
        in_specs=[
            vec(lq1), vec(lk1), vec(lq2), vec(lk2), vec(subln),
            pl.BlockSpec((None, None, ATTN_Q_TILES, LANES, tc), lambda s, h, i: (s, h, i, 0, 0)),
            pl.BlockSpec((None, None, seq_len, LANES), lambda s, h, i: (s, h, 0, 0)),
            pl.BlockSpec((None, None, n_tiles, v_rows, tc), lambda s, h, i: (s, h, 0, 0, 0)),
        ],
        out_specs=pl.BlockSpec((None, tq, LANES), lambda s, h, i: (s, i, h)),
        out_shape=jax.ShapeDtypeStruct((n_seq, seq_len, n_heads * V_HEAD_DIM), BF16),
        scratch_shapes=[pltpu.VMEM((2, v_rows, tq), F32), pltpu.VMEM((2, 2, tc, tq), F32)],
        compiler_params=_cparams(("parallel", "parallel", "arbitrary")),
        name="diff_attention",
    )(lq1, lk1, lq2, lk2, subln, qt, k, vt)


def _out_proj_kernel(z_ref, a_ref, gh_ref, wt_ref, wb_ref, gp_ref, x_ref, o_ref):
    hn = _rms(z_ref[...], gh_ref[...]).astype(BF16)
    mix = (jnp.dot(hn, wt_ref[...], preferred_element_type=F32)
           + jnp.dot(a_ref[...], wb_ref[...], preferred_element_type=F32))
    o_ref[...] = x_ref[...] + _rms(mix, gp_ref[...])


def _out_proj(z, at, g_hy, w_out, g_post, x, tm=512):
    t_tok = x.shape[0]
    row = lambda w: pl.BlockSpec((tm, w), lambda i: (i, 0))
    return pl.pallas_call(
        _out_proj_kernel,
        grid=(t_tok // tm,),
        in_specs=[
            row(HYENA_WIDTH), row(ATTN_WIDTH),
            pl.BlockSpec((1, HYENA_WIDTH), lambda i: (0, 0)),
            pl.BlockSpec((HYENA_WIDTH, D_MODEL), lambda i: (0, 0)),
            pl.BlockSpec((ATTN_WIDTH, D_MODEL), lambda i: (1, 0)),
            pl.BlockSpec((1, D_MODEL), lambda i: (0, 0)),
            row(D_MODEL),
        ],
        out_specs=row(D_MODEL),
        out_shape=jax.ShapeDtypeStruct(x.shape, F32),
        compiler_params=_cparams(("parallel",)),
        name="out_proj",
    )(z, at, g_hy, w_out, w_out, g_post, x)


FFN_TF = 512


def _ffn_kernel(x_ref, gpre_ref, wg_ref, wu_ref, wd_ref, gpost_ref, o_ref, h_scr, acc_scr):
    f = pl.program_id(1)

    @pl.when(f == 0)
    def _():
        h_scr[...] = _rms(x_ref[...], gpre_ref[...]).astype(BF16)
        acc_scr[...] = jnp.zeros(acc_scr.shape, F32)

    h = h_scr[...]
    gate = jnp.dot(h, wg_ref[...], preferred_element_type=F32)
    up = jnp.dot(h, wu_ref[...], preferred_element_type=F32)
    act = (gate * jax.nn.sigmoid(gate) * up).astype(BF16)
    acc_scr[...] += jnp.dot(act, wd_ref[...], preferred_element_type=F32)

    @pl.when(f == pl.num_programs(1) - 1)
    def _():
        o_ref[...] = x_ref[...] + _rms(acc_scr[...], gpost_ref[...])


def _ffn(x, g_pre, w_gate, w_up, w_down, g_post, tm=512):
    t_tok = x.shape[0]
    return pl.pallas_call(
        _ffn_kernel,
        grid=(t_tok // tm, D_FF // FFN_TF),
        in_specs=[
            pl.BlockSpec((tm, D_MODEL), lambda i, f: (i, 0)),
            pl.BlockSpec((1, D_MODEL), lambda i, f: (0, 0)),
            pl.BlockSpec((D_MODEL, FFN_TF), lambda i, f: (0, f)),
            pl.BlockSpec((D_MODEL, FFN_TF), lambda i, f: (0, f)),
            pl.BlockSpec((FFN_TF, D_MODEL), lambda i, f: (f, 0)),
            pl.BlockSpec((1, D_MODEL), lambda i, f: (0, 0)),
        ],
        out_specs=pl.BlockSpec((tm, D_MODEL), lambda i, f: (i, 0)),
        out_shape=jax.ShapeDtypeStruct(x.shape, F32),
        scratch_shapes=[pltpu.VMEM((tm, D_MODEL), BF16), pltpu.VMEM((tm, D_MODEL), F32)],
        compiler_params=_cparams(("parallel", "arbitrary")),
        name="ffn",
    )(x, g_pre, w_gate, w_up, w_down, g_post)


def _rope_tables(seq_len):
    inv_freq = 1.0 / (ROPE_THETA ** (jnp.arange(0, HEAD_DIM, 2, dtype=F32) / HEAD_DIM))
    ang = jnp.arange(seq_len, dtype=F32)[:, None] * inv_freq[None, :]
    cos, sin = jnp.cos(ang), jnp.sin(ang)
    zero = jnp.zeros_like(sin)
    reps = LANES // HEAD_DIM
    cos_t = jnp.tile(jnp.concatenate([cos, cos], axis=1), (1, reps))
    sa_t = jnp.tile(jnp.concatenate([-sin, zero], axis=1), (1, reps))
    sb_t = jnp.tile(jnp.concatenate([zero, sin], axis=1), (1, reps))
    return cos_t, sa_t, sb_t


def _hyena_conv(u_v, v_off, xg, xg_off, kf, order, bias, tabs):
    m_pair, _, m_inv, fb, fbi = tabs
    a1 = _dft_a(u_v, m_pair, v_off)
    p, r, nbc, sl, c = a1.shape
    g = _dft_b_conv(a1.reshape(p, r, nbc * sl, c), kf, order, fb, fbi)
    return _dft_c_gate(g.reshape(a1.shape), m_inv, xg, xg_off, u_v, v_off, bias)


def _filter_spectra(li, seq_len, tabs, zfeat, delta, w):
    (filt_w1, filt_b1, filt_f1, filt_w2, filt_b2, filt_f2, filt_w3, filt_b3, filt_f3, filt_w4) = w[7:17]
    row = lambda a: a[li].reshape(1, -1)
    r = DFT_RADIX
    nbc = r // SUBLANES
    pad_w1 = jnp.pad(filt_w1[li], ((0, FILTER_HIDDEN - FILTER_EMB), (0, 0)))
    kt = _filters(zfeat, pad_w1, row(filt_b1), row(filt_f1), filt_w2[li], row(filt_b2), row(filt_f2),
                  filt_w3[li], row(filt_b3), row(filt_f3), filt_w4[li].astype(BF16), delta, seq_len)
    ka = _dft_a(kt.reshape(2, r, nbc, SUBLANES, HYENA_WIDTH), tabs[1], 0)
    return _dft_b_fwd(ka.reshape(2, r, r, HYENA_WIDTH), tabs[3], 1.0 / (2 * seq_len))


def _layer(x, li, n_seq, seq_len, rope, tabs, kf, w):
    (norm_mix_pre, norm_mix_post, norm_ffn_pre, norm_ffn_post, w_in, conv_w, conv_b,
     _, _, _, _, _, _, _, _, _, _,
     hyena_bias, hyena_norm, lam_q1, lam_k1, lam_q2, lam_k2, subln,
     w_out, w_gate, w_up, w_down) = w
    lam_init = 0.8 - 0.6 * math.exp(-0.3 * li)
    row = lambda a: a[li].reshape(1, -1)
    r = DFT_RADIX
    nbc = r // SUBLANES
    n_pair = n_seq // 2

    hy, q, k, v = _in_proj(x, row(norm_mix_pre), w_in[li], *rope, n_seq, seq_len)

    cw = jnp.pad(conv_w[li], ((0, SUBLANES - conv_w.shape[1]), (0, 0)))
    u = _short_conv(hy.reshape(n_seq, seq_len, HYENA_IN), cw, row(conv_b))
    u6 = u.reshape(n_pair, 2, r // 2, nbc, SUBLANES, HYENA_IN)
    z1 = _hyena_conv(u6, 2 * HYENA_WIDTH, u6, 0, kf, 0, hyena_bias[li, 0].reshape(1, -1), tabs)
    z2 = _hyena_conv(z1, 0, u6, HYENA_WIDTH, kf, 1, hyena_bias[li, 1].reshape(1, -1), tabs)

    at = _attention(q, k, v, row(lam_q1), row(lam_k1), row(lam_q2), row(lam_k2), row(subln), lam_init)

    x = _out_proj(z2.reshape(n_seq * seq_len, HYENA_WIDTH), at.reshape(n_seq * seq_len, ATTN_WIDTH),
                  row(hyena_norm), w_out[li], row(norm_mix_post), x)
    return _ffn(x, row(norm_ffn_pre), w_gate[li], w_up[li], w_down[li], row(norm_ffn_post))


def kernel(x_prompt, x_sample, norm_mix_pre, norm_mix_post, norm_ffn_pre, norm_ffn_post, w_in, conv_w, conv_b, filt_w1, filt_b1, filt_f1, filt_w2, filt_b2, filt_f2, filt_w3, filt_b3, filt_f3, filt_w4, hyena_bias, hyena_norm, lam_q1, lam_k1, lam_q2, lam_k2, subln, w_out, w_gate, w_up, w_down):
    assert x_prompt.shape[1:] == x_sample.shape[1:]
    n_p, seq_len, d = x_prompt.shape
    n_s = x_sample.shape[0]
    assert n_p % 2 == 0 and n_s % 2 == 0 and seq_len == DFT_RADIX * DFT_RADIX // 2 and d == D_MODEL
    xs = [x_prompt.reshape(n_p * seq_len, d), x_sample.reshape(n_s * seq_len, d)]
    w = (norm_mix_pre, norm_mix_post, norm_ffn_pre, norm_ffn_post, w_in.astype(BF16), conv_w, conv_b,
         filt_w1, filt_b1, filt_f1, filt_w2, filt_b2, filt_f2, filt_w3, filt_b3, filt_f3, filt_w4,
         hyena_bias, hyena_norm, lam_q1, lam_k1, lam_q2, lam_k2, subln,
         w_out.astype(BF16), w_gate.astype(BF16), w_up.astype(BF16), w_down.astype(BF16))
    rope = _rope_tables(seq_len)
    tabs = _dft_tables(seq_len)
    zfeat = _filter_features(seq_len)
    delta = jnp.abs(jnp.linspace(MIN_DECAY, MAX_DECAY, HYENA_WIDTH, dtype=F32)).reshape(1, -1)
    for li in range(DEPTH):
        kf = _filter_spectra(li, seq_len, tabs, zfeat, delta, w)
        xs = [_layer(x, li, x.shape[0] // seq_len, seq_len, rope, tabs, kf, w) for x in xs]
    return (xs[0].reshape(n_p, seq_len, d), xs[1].reshape(n_s, seq_len, d))
```

```python
import functools
import math

import jax
import jax.numpy as jnp
from jax import lax
from jax.experimental import pallas as pl
from jax.experimental.pallas import tpu as pltpu

F32 = jnp.float32
BF16 = jnp.bfloat16

D_MODEL = 2048
DEPTH = 4
HYENA_WIDTH = 1024
ATTN_WIDTH = 1024
N_HEADS = 8
HEAD_DIM = 64
V_HEAD_DIM = 128
HYENA_IN = 3 * HYENA_WIDTH
IN_WIDTH = HYENA_IN + 3 * ATTN_WIDTH
FILTER_EMB = 33
FILTER_BANDS = 16
FILTER_HIDDEN = 64
DECAY_TARGET = 1e-2
MIN_DECAY = math.log(DECAY_TARGET) / 0.3
MAX_DECAY = math.log(DECAY_TARGET) / 1.5
D_FF = 5632
ROPE_THETA = 10000.0
EPS = 1e-6
LOG2E = math.log2(math.e)

LANES = 128
SUBLANES = 8
DFT_RADIX = 128
VMEM_LIMIT = 56 * 1024 * 1024


def _cparams(sem):
    return pltpu.CompilerParams(dimension_semantics=sem, vmem_limit_bytes=VMEM_LIMIT)


def _rms(xf, g):
    return xf * lax.rsqrt(jnp.mean(xf * xf, axis=-1, keepdims=True) + EPS) * g


IN_TN = 1024
HY_TILES = HYENA_IN // IN_TN


def _in_proj_kernel(x_ref, g_ref, w_ref, cos_ref, sa_ref, sb_ref,
                    hy_ref, q_ref, k_ref, v_ref, h_scr):
    j = pl.program_id(1)

    @pl.when(j == 0)
    def _():
        h_scr[...] = _rms(x_ref[...], g_ref[...]).astype(BF16)

    res = jnp.dot(h_scr[...], w_ref[...], preferred_element_type=F32)

    @pl.when(j < HY_TILES)
    def _():
        hy_ref[...] = res

    def rope(sl):
        return (sl * cos_ref[...] + pltpu.roll(sl, LANES - HEAD_DIM // 2, 1) * sa_ref[...]
                + pltpu.roll(sl, HEAD_DIM // 2, 1) * sb_ref[...])

    n_tiles = q_ref.shape[1]
    tt = q_ref.shape[3]

    @pl.when(j == HY_TILES)
    def _():
        for h in range(N_HEADS):
            ro = rope(res[:, h * LANES:(h + 1) * LANES]) * (HEAD_DIM ** -0.5 * LOG2E)
            for t in range(n_tiles):
                q_ref[h, t] = ro[t * tt:(t + 1) * tt].T.astype(BF16)

    @pl.when(j == HY_TILES + 1)
    def _():
        for h in range(N_HEADS):
            k_ref[h] = rope(res[:, h * LANES:(h + 1) * LANES]).astype(BF16)

    @pl.when(j == HY_TILES + 2)
    def _():
        ones = jnp.ones((V_ONES_ROWS, tt), BF16)
        for h in range(N_HEADS):
            for t in range(n_tiles):
                v_ref[h, t, :V_HEAD_DIM] = res[t * tt:(t + 1) * tt, h * LANES:(h + 1) * LANES].T.astype(BF16)
                v_ref[h, t, V_HEAD_DIM:] = ones


ATTN_TILE = 512
V_ONES_ROWS = 16


def _in_proj(x, g, w, cos_t, sa_t, sb_t, n_seq, seq_len, tm=512):
    t_tok = x.shape[0]
    lt = seq_len // tm
    tt = ATTN_TILE
    head_spec = pl.BlockSpec((None, N_HEADS, tm, LANES), lambda i, j: (i // lt, 0, i % lt, 0))
    tile = lambda rows: pl.BlockSpec((None, N_HEADS, tm // tt, rows, tt), lambda i, j: (i // lt, 0, i % lt, 0, 0))
    tab_spec = pl.BlockSpec((tm, LANES), lambda i, j: (i % lt, 0))
    head_shape = jax.ShapeDtypeStruct((n_seq, N_HEADS, seq_len, LANES), BF16)
    tile_shape = lambda rows: jax.ShapeDtypeStruct((n_seq, N_HEADS, seq_len // tt, rows, tt), BF16)
    v_rows = V_HEAD_DIM + V_ONES_ROWS
    return pl.pallas_call(
        _in_proj_kernel,
        grid=(t_tok // tm, IN_WIDTH // IN_TN),
        in_specs=[
            pl.BlockSpec((tm, D_MODEL), lambda i, j: (i, 0)),
            pl.BlockSpec((1, D_MODEL), lambda i, j: (0, 0)),
            pl.BlockSpec((None, D_MODEL, IN_TN), lambda i, j: (j, 0, 0)),
            tab_spec, tab_spec, tab_spec,
        ],
        out_specs=[
            pl.BlockSpec((tm, IN_TN), lambda i, j: (i, jnp.minimum(j, HY_TILES - 1))),
            tile(LANES), head_spec, tile(v_rows),
        ],
        out_shape=[jax.ShapeDtypeStruct((t_tok, HYENA_IN), F32), tile_shape(LANES), head_shape, tile_shape(v_rows)],
        scratch_shapes=[pltpu.VMEM((tm, D_MODEL), BF16)],
        compiler_params=_cparams(("parallel", "arbitrary")),
        name="in_proj",
    )(x, g, w, cos_t, sa_t, sb_t)


def _short_conv_kernel(x_ref, w_ref, b_ref, o_ref):
    x = x_ref[...]
    n = x.shape[0]
    row = lax.broadcasted_iota(jnp.int32, x.shape, 0)
    prev = jnp.where(row == 0, 0.0, pltpu.roll(x, 1, 0))
    nxt = jnp.where(row == n - 1, 0.0, pltpu.roll(x, n - 1, 0))
    o_ref[...] = b_ref[...] + prev * w_ref[0:1, :] + x * w_ref[1:2, :] + nxt * w_ref[2:3, :]


def _short_conv(hy, w, b):
    n_seq, seq_len, width = hy.shape
    return pl.pallas_call(
        _short_conv_kernel,
        grid=(n_seq, width // LANES),
        in_specs=[
            pl.BlockSpec((None, seq_len, LANES), lambda s, c: (s, 0, c)),
            pl.BlockSpec((SUBLANES, LANES), lambda s, c: (0, c)),
            pl.BlockSpec((1, LANES), lambda s, c: (0, c)),
        ],
        out_specs=pl.BlockSpec((None, seq_len, LANES), lambda s, c: (s, 0, c)),
        out_shape=jax.ShapeDtypeStruct(hy.shape, F32),
        compiler_params=_cparams(("parallel", "parallel")),
        name="short_conv",
    )(hy, w, b)


def _filter_kernel(z_ref, w1_ref, b1_ref, f1_ref, w2_ref, b2_ref, f2_ref,
                   w3_ref, b3_ref, f3_ref, w4_ref, delta_ref, o_ref, *, seq_len):
    hp = lax.Precision.HIGHEST
    z = z_ref[...]
    h = jnp.sin(f1_ref[...] * (jnp.dot(z, w1_ref[...], precision=hp, preferred_element_type=F32) + b1_ref[...]))
    h = jnp.sin(f2_ref[...] * (jnp.dot(h, w2_ref[...], precision=hp, preferred_element_type=F32) + b2_ref[...]))
    h = jnp.sin(f3_ref[...] * (jnp.dot(h, w3_ref[...], precision=hp, preferred_element_type=F32) + b3_ref[...]))
    rb = z.shape[0]
    n = pl.program_id(0) * rb + lax.broadcasted_iota(jnp.int32, (rb, 1), 0)
    m_fwd = (n < seq_len).astype(F32)
    m_bwd = jnp.logical_or(n == 0, n > seq_len).astype(F32)
    decay = jnp.exp(-z[:, 0:1] * delta_ref[...])
    cw = HYENA_WIDTH
    hb = h.astype(BF16)
    for order in range(2):
        base = order * 2 * cw
        fwd = jnp.dot(hb, w4_ref[:, base:base + cw], preferred_element_type=F32)
        bwd = jnp.dot(hb, w4_ref[:, base + cw:base + 2 * cw], preferred_element_type=F32)
        o_ref[order] = (m_fwd * fwd + m_bwd * bwd) * decay


def _filters(zfeat, w1, b1, f1, w2, b2, f2, w3, b3, f3, w4, delta, seq_len, rb=512):
    n2 = zfeat.shape[0]
    full = lambda a: pl.BlockSpec(a.shape, lambda i: (0,) * a.ndim)
    args = (w1, b1, f1, w2, b2, f2, w3, b3, f3, w4, delta)
    return pl.pallas_call(
        functools.partial(_filter_kernel, seq_len=seq_len),
        grid=(n2 // rb,),
        in_specs=[pl.BlockSpec((rb, zfeat.shape[1]), lambda i: (i, 0))] + [full(a) for a in args],
        out_specs=pl.BlockSpec((2, rb, HYENA_WIDTH), lambda i: (0, i, 0)),
        out_shape=jax.ShapeDtypeStruct((2, n2, HYENA_WIDTH), F32),
        compiler_params=_cparams(("parallel",)),
        name="hyena_filter",
    )(zfeat, *args)


DFT_CB = 256
DFT_KK = 8


U32 = jnp.uint32


def _pack_complex(re, im):
    def rounded(x):
        b = pltpu.bitcast(x, U32)
        return b + U32(0x7FFF) + ((b >> 16) & U32(1))
    return (rounded(re) & U32(0xFFFF0000)) | (rounded(im) >> 16)


def _unpack_complex(w):
    return pltpu.bitcast(w & U32(0xFFFF0000), F32), pltpu.bitcast(w << 16, F32)


def _dft_a_kernel(u_ref, m_ref, o_ref):
    cb = u_ref.shape[-1]
    x = u_ref[...].reshape(DFT_RADIX * SUBLANES, cb).astype(BF16)
    r = jnp.dot(m_ref[...], x, preferred_element_type=F32)
    half = DFT_RADIX * SUBLANES
    o_ref[...] = _pack_complex(r[:half], r[half:]).reshape(o_ref.shape)


def _dft_a(u6, m_a, ch_off):
    g = u6.shape[0]
    lead = u6.shape[1:-3]
    nbc = u6.shape[-3]
    cout = HYENA_WIDTH
    blk_in = (None,) + lead + (None, SUBLANES, DFT_CB)
    zeros = (0,) * len(lead)
    return pl.pallas_call(
        _dft_a_kernel,
        grid=(nbc, g, cout // DFT_CB),
        in_specs=[
            pl.BlockSpec(blk_in, lambda bc, p, c: (p,) + zeros + (bc, 0, c + ch_off // DFT_CB)),
            pl.BlockSpec((None,) + m_a.shape[1:], lambda bc, p, c: (bc, 0, 0)),
        ],
        out_specs=pl.BlockSpec((None, DFT_RADIX, None, SUBLANES, DFT_CB),
                               lambda bc, p, c: (p, 0, bc, 0, c)),
        out_shape=jax.ShapeDtypeStruct((g, DFT_RADIX, nbc, SUBLANES, cout), U32),
        compiler_params=_cparams(("arbitrary", "arbitrary", "arbitrary")),
        name="dft_stage_a",
    )(u6, m_a)


def _dft_b_conv_kernel(a_ref, kf_ref, fb_ref, fbi_ref, g_ref):
    r = DFT_RADIX
    for i in range(DFT_KK):
        s = jnp.concatenate(_unpack_complex(a_ref[i]), axis=0).astype(BF16)
        x = jnp.dot(fb_ref[...], s, preferred_element_type=F32)
        xr, xi = x[:r], x[r:]
        kr, ki = kf_ref[0, i], kf_ref[1, i]
        y = jnp.concatenate([xr * kr - xi * ki, xr * ki + xi * kr], axis=0).astype(BF16)
        gg = jnp.dot(fbi_ref[...], y, preferred_element_type=F32)
        g_ref[i] = _pack_complex(gg[:r], gg[r:])


def _dft_b_conv(a1, kf, order, fb, fbi):
    p, r, _, c = a1.shape
    blk = pl.BlockSpec((None, DFT_KK, r, DFT_CB), lambda k, cc, pp: (pp, k, 0, cc))
    mat = pl.BlockSpec((2 * r, 2 * r), lambda k, cc, pp: (0, 0))
    return pl.pallas_call(
        _dft_b_conv_kernel,
        grid=(r // DFT_KK, c // DFT_CB, p),
        in_specs=[blk, pl.BlockSpec((None, 2, DFT_KK, r, DFT_CB), lambda k, cc, pp: (order, 0, k, 0, cc)),
                  mat, mat],
        out_specs=blk,
        out_shape=jax.ShapeDtypeStruct(a1.shape, U32),
        compiler_params=_cparams(("parallel", "parallel", "arbitrary")),
        name="dft_stage_b_conv",
    )(a1, kf, fb, fbi)


def _dft_b_fwd_kernel(a_ref, fb_ref, o_ref, *, scale):
    r = DFT_RADIX
    for i in range(DFT_KK):
        s = jnp.concatenate(_unpack_complex(a_ref[i]), axis=0).astype(BF16)
        x = jnp.dot(fb_ref[...], s, preferred_element_type=F32) * scale
        o_ref[0, i] = x[:r]
        o_ref[1, i] = x[r:]


def _dft_b_fwd(a1, fb, scale):
    p, r, _, c = a1.shape
    return pl.pallas_call(
        functools.partial(_dft_b_fwd_kernel, scale=scale),
        grid=(r // DFT_KK, c // DFT_CB, p),
        in_specs=[pl.BlockSpec((None, DFT_KK, r, DFT_CB), lambda k, cc, pp: (pp, k, 0, cc)),
                  pl.BlockSpec((2 * r, 2 * r), lambda k, cc, pp: (0, 0))],
        out_specs=pl.BlockSpec((None, 2, DFT_KK, r, DFT_CB), lambda k, cc, pp: (pp, 0, k, 0, cc)),
        out_shape=jax.ShapeDtypeStruct((p, 2, r, r, c), F32),
        compiler_params=_cparams(("parallel", "parallel", "arbitrary")),
        name="dft_stage_b_filter",
    )(a1, fb)


def _dft_c_gate_kernel(g_ref, m_ref, x_ref, v_ref, bias_ref, o_ref):
    cb = g_ref.shape[-1]
    gr, gi = _unpack_complex(g_ref[...].reshape(DFT_RADIX * SUBLANES, cb))
    g = jnp.concatenate([gr, gi], axis=0).astype(BF16)
    y = jnp.dot(m_ref[...], g, preferred_element_type=F32).reshape(o_ref.shape)
    o_ref[...] = x_ref[...] * (y + v_ref[...] * bias_ref[...])


def _dft_c_gate(g5, m_c, xg, xg_off, v, v_off, bias):
    p, r, nbc, _, c = g5.shape
    half = r // 2
    tspec = lambda off: pl.BlockSpec((None, 2, half, None, SUBLANES, DFT_CB),
                                     lambda bc, pp, cc: (pp, 0, 0, bc, 0, cc + off // DFT_CB))
    return pl.pallas_call(
        _dft_c_gate_kernel,
        grid=(nbc, p, c // DFT_CB),
        in_specs=[
            pl.BlockSpec((None, r, None, SUBLANES, DFT_CB), lambda bc, pp, cc: (pp, 0, bc, 0, cc)),
            pl.BlockSpec((None,) + m_c.shape[1:], lambda bc, pp, cc: (bc, 0, 0)),
            tspec(xg_off), tspec(v_off),
            pl.BlockSpec((1, DFT_CB), lambda bc, pp, cc: (0, cc)),
        ],
        out_specs=tspec(0),
        out_shape=jax.ShapeDtypeStruct((p, 2, half, nbc, SUBLANES, c), F32),
        compiler_params=_cparams(("arbitrary", "arbitrary", "arbitrary")),
        name="dft_stage_c_gate",
    )(g5, m_c, xg, v, bias)


def _dft_tables(seq_len):
    r = DFT_RADIX
    n = 2 * seq_len
    nbc = r // SUBLANES
    i32 = jnp.int32
    nt = r * SUBLANES
    idx = jnp.arange(r, dtype=i32)
    alpha = ((idx[:, None] * idx[None, :]) % r).astype(F32) * (2.0 * math.pi / r)
    beta = ((idx[:, None] * idx[None, :]) % n).astype(F32) * (2.0 * math.pi / n)
    ca, sa, cb, sb = jnp.cos(alpha), jnp.sin(alpha), jnp.cos(beta), jnp.sin(beta)

    def by_time(t, pair):
        if pair:
            t = jnp.broadcast_to(t[:, None, None, :r // 2, None], (r, SUBLANES, 2, r // 2, SUBLANES))
        else:
            t = jnp.broadcast_to(t[:, None, :, None], (r, SUBLANES, r, SUBLANES))
        return t.reshape(nt, nt)

    def by_offset(t):
        return jnp.transpose(t.reshape(r, nbc, SUBLANES), (1, 0, 2)).reshape(nbc, nt, 1)

    fi = lax.broadcasted_iota(i32, (nt, nt), 0)
    ti = lax.broadcasted_iota(i32, (nt, nt), 1)
    same_row = (fi % SUBLANES) == (ti % SUBLANES)
    first_seq = ti < nt // 2

    def cos_sin(pair):
        ac, as_, bc_, bs_ = by_time(ca, pair), by_time(sa, pair), by_offset(cb), by_offset(sb)
        c = jnp.where(same_row, ac * bc_ - as_ * bs_, 0.0)
        s = jnp.where(same_row, as_ * bc_ + ac * bs_, 0.0)
        return c, s

    c, s = cos_sin(True)
    re_rows, im_rows = jnp.where(first_seq, c, s), jnp.where(first_seq, -s, c)
    m_pair = jnp.concatenate([re_rows, im_rows], axis=1).astype(BF16)
    m_inv = jnp.concatenate([jnp.swapaxes(re_rows, 1, 2), jnp.swapaxes(im_rows, 1, 2)], axis=2).astype(BF16)
    c, s = cos_sin(False)
    m_real = jnp.concatenate([c, -s], axis=1).astype(BF16)
    kb = idx
    angb = ((kb[:, None] * kb[None, :]) % r).astype(F32) * (2.0 * math.pi / r)
    cb_, sb_ = jnp.cos(angb), jnp.sin(angb)
    fb = jnp.block([[cb_, sb_], [-sb_, cb_]]).astype(BF16)
    fbi = jnp.block([[cb_, -sb_], [sb_, cb_]]).astype(BF16)
    return m_pair, m_real, m_inv, fb, fbi


def _filter_features(seq_len):
    n2 = 2 * seq_len
    n = jnp.arange(n2)
    pos = jnp.where(n <= seq_len, n, n2 - n)
    pos = jnp.minimum(pos, seq_len - 1)
    t_lin = jnp.linspace(0.0, 1.0, seq_len, dtype=F32)
    omega = 2.0 * math.pi * jnp.arange(seq_len, dtype=F32) / seq_len
    bands = jnp.linspace(1e-4, FILTER_BANDS - 1, FILTER_BANDS, dtype=F32)
    phase = omega[:, None] * bands[None, :]
    z = jnp.concatenate([t_lin[:, None], jnp.cos(phase), -jnp.sin(phase)], axis=-1)
    z = jnp.pad(z, ((0, 0), (0, FILTER_HIDDEN - FILTER_EMB)))
    return z[pos]


ATTN_Q_TILES = 2


def _attn_kernel(lq1_ref, lk1_ref, lq2_ref, lk2_ref, subln_ref, qt_ref, k_ref, vt_ref, o_ref,
                 acc_scr, s_scr, *, lam_init):
    qt = jnp.concatenate([qt_ref[t] for t in range(qt_ref.shape[0])], axis=1)
    tq = qt.shape[1]
    tc = vt_ref.shape[2]
    row = lax.broadcasted_iota(jnp.int32, qt.shape, 0)
    zero = jnp.zeros_like(qt)
    qm = (jnp.where(row < HEAD_DIM, qt, zero), jnp.where(row >= HEAD_DIM, qt, zero))
    acc_scr[...] = jnp.zeros(acc_scr.shape, F32)
    n_chunks = vt_ref.shape[0]

    def scores(c, slot):
        k_c = k_ref[pl.ds(pl.multiple_of(c * tc, tc), tc), :]
        cms = []
        for i in range(2):
            st = jnp.dot(k_c, qm[i], preferred_element_type=F32)
            s_scr[slot, i] = st
            cms.append(jnp.max(st, axis=0, keepdims=True))
        return tuple(cms)

    def softmax_pv(c, slot, cms, ms):
        vt_c = vt_ref[c]
        new_m = []
        for i in range(2):
            m_new = jnp.maximum(ms[i], cms[i])
            alpha = jnp.exp2(ms[i] - m_new)
            p = jnp.exp2(s_scr[slot, i] - m_new)
            acc_scr[i] = alpha * acc_scr[i] + jnp.dot(vt_c, p.astype(BF16), preferred_element_type=F32)
            new_m.append(m_new)
        return tuple(new_m)

    def pair(j, carry):
        cm, ms = carry
        c = 2 * j
        cm1 = scores(c + 1, 1)
        ms = softmax_pv(c, 0, cm, ms)
        cm2 = scores(c + 2, 0)
        ms = softmax_pv(c + 1, 1, cm1, ms)
        return cm2, ms

    neg = jnp.full((1, tq), -jnp.inf, F32)
    cm, ms = lax.fori_loop(0, n_chunks // 2 - 1, pair, (scores(0, 0), (neg, neg)), unroll=True)
    cm1 = scores(n_chunks - 1, 1)
    ms = softmax_pv(n_chunks - 2, 0, cm, ms)
    ms = softmax_pv(n_chunks - 1, 1, cm1, ms)

    lam = (jnp.exp(jnp.sum(lq1_ref[...] * lk1_ref[...], axis=1, keepdims=True))
           - jnp.exp(jnp.sum(lq2_ref[...] * lk2_ref[...], axis=1, keepdims=True)) + lam_init)
    vd = V_HEAD_DIM
    ot = (acc_scr[0, :vd] * (1.0 / acc_scr[0, vd:vd + 1])
          - lam * (acc_scr[1, :vd] * (1.0 / acc_scr[1, vd:vd + 1])))
    o = _rms(ot.T, subln_ref[...]) * (1.0 - lam_init)
    o_ref[...] = o.astype(o_ref.dtype)


def _attention(qt, k, vt, lq1, lk1, lq2, lk2, subln, lam_init):
    n_seq, n_heads, n_tiles, _, tc = qt.shape
    seq_len = k.shape[2]
    tq = ATTN_Q_TILES * tc
    v_rows = vt.shape[3]
    vec = lambda a: pl.BlockSpec(a.shape, lambda s, h, i: (0, 0))
    return pl.pallas_call(
        functools.partial(_attn_kernel, lam_init=lam_init),
        grid=(n_seq, n_heads, seq_len // tq),
        in_specs=[
            vec(lq1), vec(lk1), vec(lq2), vec(lk2), vec(subln),
            pl.BlockSpec((None, None, ATTN_Q_TILES, LANES, tc), lambda s, h, i: (s, h, i, 0, 0)),
            pl.BlockSpec((None, None, seq_len, LANES), lambda s, h, i: (s, h, 0, 0)),
            pl.BlockSpec((None, None, n_tiles, v_rows, tc), lambda s, h, i: (s, h, 0, 0, 0)),
        ],
        out_specs=pl.BlockSpec((None, tq, LANES), lambda s, h, i: (s, i, h)),
        out_shape=jax.ShapeDtypeStruct((n_seq, seq_len, n_heads * V_HEAD_DIM), BF16),
        scratch_shapes=[pltpu.VMEM((2, v_rows, tq), F32), pltpu.VMEM((2, 2, tc, tq), F32)],
        compiler_params=_cparams(("parallel", "parallel", "arbitrary")),
        name="diff_attention",
    )(lq1, lk1, lq2, lk2, subln, qt, k, vt)


def _out_proj_kernel(z_ref, a_ref, gh_ref, wt_ref, wb_ref, gp_ref, x_ref, o_ref):
    hn = _rms(z_ref[...], gh_ref[...]).astype(BF16)
    mix = (jnp.dot(hn, wt_ref[...], preferred_element_type=F32)
           + jnp.dot(a_ref[...], wb_ref[...], preferred_element_type=F32))
    o_ref[...] = x_ref[...] + _rms(mix, gp_ref[...])


def _out_proj(z, at, g_hy, w_out, g_post, x, tm=512):
    t_tok = x.shape[0]
    row = lambda w: pl.BlockSpec((tm, w), lambda i: (i, 0))
    return pl.pallas_call(
        _out_proj_kernel,
        grid=(t_tok // tm,),
        in_specs=[
            row(HYENA_WIDTH), row(ATTN_WIDTH),
            pl.BlockSpec((1, HYENA_WIDTH), lambda i: (0, 0)),
            pl.BlockSpec((HYENA_WIDTH, D_MODEL), lambda i: (0, 0)),
            pl.BlockSpec((ATTN_WIDTH, D_MODEL), lambda i: (1, 0)),
            pl.BlockSpec((1, D_MODEL), lambda i: (0, 0)),
            row(D_MODEL),
        ],
        out_specs=row(D_MODEL),
        out_shape=jax.ShapeDtypeStruct(x.shape, F32),
        compiler_params=_cparams(("parallel",)),
        name="out_proj",
    )(z, at, g_hy, w_out, w_out, g_post, x)


FFN_TF = 512


def _ffn_kernel(x_ref, gpre_ref, wg_ref, wu_ref, wd_ref, gpost_ref, o_ref, h_scr, acc_scr):
    f = pl.program_id(1)

    @pl.when(f == 0)
    def _():
        h_scr[...] = _rms(x_ref[...], gpre_ref[...]).astype(BF16)
        acc_scr[...] = jnp.zeros(acc_scr.shape, F32)

    h = h_scr[...]
    gate = jnp.dot(h, wg_ref[...], preferred_element_type=F32)
    up = jnp.dot(h, wu_ref[...], preferred_element_type=F32)
    act = (gate * jax.nn.sigmoid(gate) * up).astype(BF16)
    acc_scr[...] += jnp.dot(act, wd_ref[...], preferred_element_type=F32)

    @pl.when(f == pl.num_programs(1) - 1)
    def _():
        o_ref[...] = x_ref[...] + _rms(acc_scr[...], gpost_ref[...])


def _ffn(x, g_pre, w_gate, w_up, w_down, g_post, tm=512):
    t_tok = x.shape[0]
    return pl.pallas_call(
        _ffn_kernel,
        grid=(t_tok // tm, D_FF // FFN_TF),
        in_specs=[
            pl.BlockSpec((tm, D_MODEL), lambda i, f: (i, 0)),
            pl.BlockSpec((1, D_MODEL), lambda i, f: (0, 0)),
            pl.BlockSpec((None, D_MODEL, FFN_TF), lambda i, f: (f, 0, 0)),
            pl.BlockSpec((None, D_MODEL, FFN_TF), lambda i, f: (f, 0, 0)),
            pl.BlockSpec((FFN_TF, D_MODEL), lambda i, f: (f, 0)),
            pl.BlockSpec((1, D_MODEL), lambda i, f: (0, 0)),
        ],
        out_specs=pl.BlockSpec((tm, D_MODEL), lambda i, f: (i, 0)),
        out_shape=jax.ShapeDtypeStruct(x.shape, F32),
        scratch_shapes=[pltpu.VMEM((tm, D_MODEL), BF16), pltpu.VMEM((tm, D_MODEL), F32)],
        compiler_params=_cparams(("parallel", "arbitrary")),
        name="ffn",
    )(x, g_pre, w_gate, w_up, w_down, g_post)


def _rope_tables(seq_len):
    inv_freq = 1.0 / (ROPE_THETA ** (jnp.arange(0, HEAD_DIM, 2, dtype=F32) / HEAD_DIM))
    ang = jnp.arange(seq_len, dtype=F32)[:, None] * inv_freq[None, :]
    cos, sin = jnp.cos(ang), jnp.sin(ang)
    zero = jnp.zeros_like(sin)
    reps = LANES // HEAD_DIM
    cos_t = jnp.tile(jnp.concatenate([cos, cos], axis=1), (1, reps))
    sa_t = jnp.tile(jnp.concatenate([-sin, zero], axis=1), (1, reps))
    sb_t = jnp.tile(jnp.concatenate([zero, sin], axis=1), (1, reps))
    return cos_t, sa_t, sb_t


def _hyena_conv(u_v, v_off, xg, xg_off, kf, order, bias, tabs):
    m_pair, _, m_inv, fb, fbi = tabs
    a1 = _dft_a(u_v, m_pair, v_off)
    p, r, nbc, sl, c = a1.shape
    g = _dft_b_conv(a1.reshape(p, r, nbc * sl, c), kf, order, fb, fbi)
    return _dft_c_gate(g.reshape(a1.shape), m_inv, xg, xg_off, u_v, v_off, bias)


def _filter_spectra(li, seq_len, tabs, zfeat, delta, w):
    (filt_w1, filt_b1, filt_f1, filt_w2, filt_b2, filt_f2, filt_w3, filt_b3, filt_f3, filt_w4) = w[7:17]
    row = lambda a: a[li].reshape(1, -1)
    r = DFT_RADIX
    nbc = r // SUBLANES
    pad_w1 = jnp.pad(filt_w1[li], ((0, FILTER_HIDDEN - FILTER_EMB), (0, 0)))
    kt = _filters(zfeat, pad_w1, row(filt_b1), row(filt_f1), filt_w2[li], row(filt_b2), row(filt_f2),
                  filt_w3[li], row(filt_b3), row(filt_f3), filt_w4[li].astype(BF16), delta, seq_len)
    ka = _dft_a(kt.reshape(2, r, nbc, SUBLANES, HYENA_WIDTH), tabs[1], 0)
    return _dft_b_fwd(ka.reshape(2, r, r, HYENA_WIDTH), tabs[3], 1.0 / (2 * seq_len))


def _layer(x, li, n_seq, seq_len, rope, tabs, kf, w):
    (norm_mix_pre, norm_mix_post, norm_ffn_pre, norm_ffn_post, w_in, conv_w, conv_b,
     _, _, _, _, _, _, _, _, _, _,
     hyena_bias, hyena_norm, lam_q1, lam_k1, lam_q2, lam_k2, subln,
     w_out, w_gate, w_up, w_down) = w
    lam_init = 0.8 - 0.6 * math.exp(-0.3 * li)
    row = lambda a: a[li].reshape(1, -1)
    r = DFT_RADIX
    nbc = r // SUBLANES
    n_pair = n_seq // 2

    hy, q, k, v = _in_proj(x, row(norm_mix_pre), w_in[li], *rope, n_seq, seq_len)

    cw = jnp.pad(conv_w[li], ((0, SUBLANES - conv_w.shape[1]), (0, 0)))
    u = _short_conv(hy.reshape(n_seq, seq_len, HYENA_IN), cw, row(conv_b))
    u6 = u.reshape(n_pair, 2, r // 2, nbc, SUBLANES, HYENA_IN)
    z1 = _hyena_conv(u6, 2 * HYENA_WIDTH, u6, 0, kf, 0, hyena_bias[li, 0].reshape(1, -1), tabs)
    z2 = _hyena_conv(z1, 0, u6, HYENA_WIDTH, kf, 1, hyena_bias[li, 1].reshape(1, -1), tabs)

    at = _attention(q, k, v, row(lam_q1), row(lam_k1), row(lam_q2), row(lam_k2), row(subln), lam_init)

    x = _out_proj(z2.reshape(n_seq * seq_len, HYENA_WIDTH), at.reshape(n_seq * seq_len, ATTN_WIDTH),
                  row(hyena_norm), w_out[li], row(norm_mix_post), x)
    return _ffn(x, row(norm_ffn_pre), w_gate[li], w_up[li], w_down[li], row(norm_ffn_post))


def kernel(x_prompt, x_sample, norm_mix_pre, norm_mix_post, norm_ffn_pre, norm_ffn_post, w_in, conv_w, conv_b, filt_w1, filt_b1, filt_f1, filt_w2, filt_b2, filt_f2, filt_w3, filt_b3, filt_f3, filt_w4, hyena_bias, hyena_norm, lam_q1, lam_k1, lam_q2, lam_k2, subln, w_out, w_gate, w_up, w_down):
    assert x_prompt.shape[1:] == x_sample.shape[1:]
    n_p, seq_len, d = x_prompt.shape
    n_s = x_sample.shape[0]
    assert n_p % 2 == 0 and n_s % 2 == 0 and seq_len == DFT_RADIX * DFT_RADIX // 2 and d == D_MODEL
    xs = [x_prompt.reshape(n_p * seq_len, d), x_sample.reshape(n_s * seq_len, d)]
    def col_tiles(wt, tn):
        dep, kk, nn = wt.shape
        return jnp.transpose(wt.astype(BF16).reshape(dep, kk, nn // tn, tn), (0, 2, 1, 3))

    w = (norm_mix_pre, norm_mix_post, norm_ffn_pre, norm_ffn_post, col_tiles(w_in, IN_TN), conv_w, conv_b,
         filt_w1, filt_b1, filt_f1, filt_w2, filt_b2, filt_f2, filt_w3, filt_b3, filt_f3, filt_w4,
         hyena_bias, hyena_norm, lam_q1, lam_k1, lam_q2, lam_k2, subln,
         w_out.astype(BF16), col_tiles(w_gate, FFN_TF), col_tiles(w_up, FFN_TF), w_down.astype(BF16))
    rope = _rope_tables(seq_len)
    tabs = _dft_tables(seq_len)
    zfeat = _filter_features(seq_len)
    delta = jnp.abs(jnp.linspace(MIN_DECAY, MAX_DECAY, HYENA_WIDTH, dtype=F32)).reshape(1, -1)
    for li in range(DEPTH):
        kf = _filter_spectra(li, seq_len, tabs, zfeat, delta, w)
        xs = [_layer(x, li, x.shape[0] // seq_len, seq_len, rope, tabs, kf, w) for x in xs]
    return (xs[0].reshape(n_p, seq_len, d), xs[1].reshape(n_s, seq_len, d))
```

```python
import functools
import math

import jax
import jax.numpy as jnp
from jax import lax
from jax.experimental import pallas as pl
from jax.experimental.pallas import tpu as pltpu

F32 = jnp.float32
BF16 = jnp.bfloat16

D_MODEL = 2048
DEPTH = 4
HYENA_WIDTH = 1024
ATTN_WIDTH = 1024
N_HEADS = 8
HEAD_DIM = 64
V_HEAD_DIM = 128
HYENA_IN = 3 * HYENA_WIDTH
IN_WIDTH = HYENA_IN + 3 * ATTN_WIDTH
FILTER_EMB = 33
FILTER_BANDS = 16
FILTER_HIDDEN = 64
DECAY_TARGET = 1e-2
MIN_DECAY = math.log(DECAY_TARGET) / 0.3
MAX_DECAY = math.log(DECAY_TARGET) / 1.5
D_FF = 5632
ROPE_THETA = 10000.0
EPS = 1e-6
LOG2E = math.log2(math.e)

LANES = 128
SUBLANES = 8
DFT_RADIX = 128
VMEM_LIMIT = 56 * 1024 * 1024


def _cparams(sem):
    return pltpu.CompilerParams(dimension_semantics=sem, vmem_limit_bytes=VMEM_LIMIT)


def _rms(xf, g):
    return xf * lax.rsqrt(jnp.mean(xf * xf, axis=-1, keepdims=True) + EPS) * g


IN_TN = 1024
HY_TILES = HYENA_IN // IN_TN


def _in_proj_kernel(x_ref, g_ref, w_ref, cos_ref, sa_ref, sb_ref,
                    hy_ref, q_ref, k_ref, v_ref, h_scr):
    j = pl.program_id(1)

    @pl.when(j == 0)
    def _():
        h_scr[...] = _rms(x_ref[...], g_ref[...]).astype(BF16)

    res = jnp.dot(h_scr[...], w_ref[...], preferred_element_type=F32)

    @pl.when(j < HY_TILES)
    def _():
        hy_ref[...] = res.astype(hy_ref.dtype)

    def rope(sl):
        return (sl * cos_ref[...] + pltpu.roll(sl, LANES - HEAD_DIM // 2, 1) * sa_ref[...]
                + pltpu.roll(sl, HEAD_DIM // 2, 1) * sb_ref[...])

    n_tiles = q_ref.shape[1]
    tt = q_ref.shape[3]

    @pl.when(j == HY_TILES)
    def _():
        for h in range(N_HEADS):
            ro = rope(res[:, h * LANES:(h + 1) * LANES]) * (HEAD_DIM ** -0.5 * LOG2E)
            for t in range(n_tiles):
                q_ref[h, t] = ro[t * tt:(t + 1) * tt].T.astype(BF16)

    @pl.when(j == HY_TILES + 1)
    def _():
        for h in range(N_HEADS):
            k_ref[h] = rope(res[:, h * LANES:(h + 1) * LANES]).astype(BF16)

    @pl.when(j == HY_TILES + 2)
    def _():
        ones = jnp.ones((V_ONES_ROWS, tt), BF16)
        for h in range(N_HEADS):
            for t in range(n_tiles):
                v_ref[h, t, :V_HEAD_DIM] = res[t * tt:(t + 1) * tt, h * LANES:(h + 1) * LANES].T.astype(BF16)
                v_ref[h, t, V_HEAD_DIM:] = ones


ATTN_TILE = 512
V_ONES_ROWS = 16


def _in_proj(x, g, w, cos_t, sa_t, sb_t, n_seq, seq_len, tm=512):
    t_tok = x.shape[0]
    lt = seq_len // tm
    tt = ATTN_TILE
    head_spec = pl.BlockSpec((None, N_HEADS, tm, LANES), lambda i, j: (i // lt, 0, i % lt, 0))
    tile = lambda rows: pl.BlockSpec((None, N_HEADS, tm // tt, rows, tt), lambda i, j: (i // lt, 0, i % lt, 0, 0))
    tab_spec = pl.BlockSpec((tm, LANES), lambda i, j: (i % lt, 0))
    head_shape = jax.ShapeDtypeStruct((n_seq, N_HEADS, seq_len, LANES), BF16)
    tile_shape = lambda rows: jax.ShapeDtypeStruct((n_seq, N_HEADS, seq_len // tt, rows, tt), BF16)
    v_rows = V_HEAD_DIM + V_ONES_ROWS
    return pl.pallas_call(
        _in_proj_kernel,
        grid=(t_tok // tm, IN_WIDTH // IN_TN),
        in_specs=[
            pl.BlockSpec((tm, D_MODEL), lambda i, j: (i, 0)),
            pl.BlockSpec((1, D_MODEL), lambda i, j: (0, 0)),
            pl.BlockSpec((None, D_MODEL, IN_TN), lambda i, j: (j, 0, 0)),
            tab_spec, tab_spec, tab_spec,
        ],
        out_specs=[
            pl.BlockSpec((tm, IN_TN), lambda i, j: (i, jnp.minimum(j, HY_TILES - 1))),
            tile(LANES), head_spec, tile(v_rows),
        ],
        out_shape=[jax.ShapeDtypeStruct((t_tok, HYENA_IN), BF16), tile_shape(LANES), head_shape, tile_shape(v_rows)],
        scratch_shapes=[pltpu.VMEM((tm, D_MODEL), BF16)],
        compiler_params=_cparams(("parallel", "arbitrary")),
        name="in_proj",
    )(x, g, w, cos_t, sa_t, sb_t)


def _short_conv_kernel(x_ref, w_ref, b_ref, o_ref):
    x = x_ref[...].astype(F32)
    n = x.shape[0]
    row = lax.broadcasted_iota(jnp.int32, x.shape, 0)
    prev = jnp.where(row == 0, 0.0, pltpu.roll(x, 1, 0))
    nxt = jnp.where(row == n - 1, 0.0, pltpu.roll(x, n - 1, 0))
    o_ref[...] = b_ref[...] + prev * w_ref[0:1, :] + x * w_ref[1:2, :] + nxt * w_ref[2:3, :]


def _short_conv(hy, w, b):
    n_seq, seq_len, width = hy.shape
    return pl.pallas_call(
        _short_conv_kernel,
        grid=(n_seq, width // LANES),
        in_specs=[
            pl.BlockSpec((None, seq_len, LANES), lambda s, c: (s, 0, c)),
            pl.BlockSpec((SUBLANES, LANES), lambda s, c: (0, c)),
            pl.BlockSpec((1, LANES), lambda s, c: (0, c)),
        ],
        out_specs=pl.BlockSpec((None, seq_len, LANES), lambda s, c: (s, 0, c)),
        out_shape=jax.ShapeDtypeStruct(hy.shape, F32),
        compiler_params=_cparams(("parallel", "parallel")),
        name="short_conv",
    )(hy, w, b)


def _filter_kernel(z_ref, w1_ref, b1_ref, f1_ref, w2_ref, b2_ref, f2_ref,
                   w3_ref, b3_ref, f3_ref, w4_ref, delta_ref, o_ref, *, seq_len):
    hp = lax.Precision.HIGHEST
    z = z_ref[...]
    h = jnp.sin(f1_ref[...] * (jnp.dot(z, w1_ref[...], precision=hp, preferred_element_type=F32) + b1_ref[...]))
    h = jnp.sin(f2_ref[...] * (jnp.dot(h, w2_ref[...], precision=hp, preferred_element_type=F32) + b2_ref[...]))
    h = jnp.sin(f3_ref[...] * (jnp.dot(h, w3_ref[...], precision=hp, preferred_element_type=F32) + b3_ref[...]))
    rb = z.shape[0]
    n = pl.program_id(0) * rb + lax.broadcasted_iota(jnp.int32, (rb, 1), 0)
    m_fwd = (n < seq_len).astype(F32)
    m_bwd = jnp.logical_or(n == 0, n > seq_len).astype(F32)
    decay = jnp.exp(-z[:, 0:1] * delta_ref[...])
    cw = HYENA_WIDTH
    hb = h.astype(BF16)
    for order in range(2):
        base = order * 2 * cw
        fwd = jnp.dot(hb, w4_ref[:, base:base + cw], preferred_element_type=F32)
        bwd = jnp.dot(hb, w4_ref[:, base + cw:base + 2 * cw], preferred_element_type=F32)
        o_ref[order] = (m_fwd * fwd + m_bwd * bwd) * decay


def _filters(zfeat, w1, b1, f1, w2, b2, f2, w3, b3, f3, w4, delta, seq_len, rb=512):
    n2 = zfeat.shape[0]
    full = lambda a: pl.BlockSpec(a.shape, lambda i: (0,) * a.ndim)
    args = (w1, b1, f1, w2, b2, f2, w3, b3, f3, w4, delta)
    return pl.pallas_call(
        functools.partial(_filter_kernel, seq_len=seq_len),
        grid=(n2 // rb,),
        in_specs=[pl.BlockSpec((rb, zfeat.shape[1]), lambda i: (i, 0))] + [full(a) for a in args],
        out_specs=pl.BlockSpec((2, rb, HYENA_WIDTH), lambda i: (0, i, 0)),
        out_shape=jax.ShapeDtypeStruct((2, n2, HYENA_WIDTH), F32),
        compiler_params=_cparams(("parallel",)),
        name="hyena_filter",
    )(zfeat, *args)


DFT_CB = 256
DFT_KK = 8


U32 = jnp.uint32


def _pack_complex(re, im):
    def rounded(x):
        b = pltpu.bitcast(x, U32)
        return b + U32(0x7FFF) + ((b >> 16) & U32(1))
    return (rounded(re) & U32(0xFFFF0000)) | (rounded(im) >> 16)


def _unpack_complex(w):
    return pltpu.bitcast(w & U32(0xFFFF0000), F32), pltpu.bitcast(w << 16, F32)


def _dft_a_kernel(u_ref, m_ref, o_ref):
    cb = u_ref.shape[-1]
    x = u_ref[...].reshape(DFT_RADIX * SUBLANES, cb).astype(BF16)
    r = jnp.dot(m_ref[...], x, preferred_element_type=F32)
    half = DFT_RADIX * SUBLANES
    o_ref[...] = _pack_complex(r[:half], r[half:]).reshape(o_ref.shape)


def _dft_a(u6, m_a, ch_off):
    g = u6.shape[0]
    lead = u6.shape[1:-3]
    nbc = u6.shape[-3]
    cout = HYENA_WIDTH
    blk_in = (None,) + lead + (None, SUBLANES, DFT_CB)
    zeros = (0,) * len(lead)
    return pl.pallas_call(
        _dft_a_kernel,
        grid=(nbc, g, cout // DFT_CB),
        in_specs=[
            pl.BlockSpec(blk_in, lambda bc, p, c: (p,) + zeros + (bc, 0, c + ch_off // DFT_CB)),
            pl.BlockSpec((None,) + m_a.shape[1:], lambda bc, p, c: (bc, 0, 0)),
        ],
        out_specs=pl.BlockSpec((None, DFT_RADIX, None, SUBLANES, DFT_CB),
                               lambda bc, p, c: (p, 0, bc, 0, c)),
        out_shape=jax.ShapeDtypeStruct((g, DFT_RADIX, nbc, SUBLANES, cout), U32),
        compiler_params=_cparams(("arbitrary", "arbitrary", "arbitrary")),
        name="dft_stage_a",
    )(u6, m_a)


def _dft_b_conv_kernel(a_ref, kf_ref, fb_ref, fbi_ref, g_ref):
    r = DFT_RADIX
    for i in range(DFT_KK):
        s = jnp.concatenate(_unpack_complex(a_ref[i]), axis=0).astype(BF16)
        x = jnp.dot(fb_ref[...], s, preferred_element_type=F32)
        xr, xi = x[:r], x[r:]
        kr, ki = kf_ref[0, i], kf_ref[1, i]
        y = jnp.concatenate([xr * kr - xi * ki, xr * ki + xi * kr], axis=0).astype(BF16)
        gg = jnp.dot(fbi_ref[...], y, preferred_element_type=F32)
        g_ref[i] = _pack_complex(gg[:r], gg[r:])


def _dft_b_conv(a1, kf, order, fb, fbi):
    p, r, _, c = a1.shape
    blk = pl.BlockSpec((None, DFT_KK, r, DFT_CB), lambda k, cc, pp: (pp, k, 0, cc))
    mat = pl.BlockSpec((2 * r, 2 * r), lambda k, cc, pp: (0, 0))
    return pl.pallas_call(
        _dft_b_conv_kernel,
        grid=(r // DFT_KK, c // DFT_CB, p),
        in_specs=[blk, pl.BlockSpec((None, 2, DFT_KK, r, DFT_CB), lambda k, cc, pp: (order, 0, k, 0, cc)),
                  mat, mat],
        out_specs=blk,
        out_shape=jax.ShapeDtypeStruct(a1.shape, U32),
        compiler_params=_cparams(("parallel", "parallel", "arbitrary")),
        name="dft_stage_b_conv",
    )(a1, kf, fb, fbi)


def _dft_b_fwd_kernel(a_ref, fb_ref, o_ref, *, scale):
    r = DFT_RADIX
    for i in range(DFT_KK):
        s = jnp.concatenate(_unpack_complex(a_ref[i]), axis=0).astype(BF16)
        x = jnp.dot(fb_ref[...], s, preferred_element_type=F32) * scale
        o_ref[0, i] = x[:r]
        o_ref[1, i] = x[r:]


def _dft_b_fwd(a1, fb, scale):
    p, r, _, c = a1.shape
    return pl.pallas_call(
        functools.partial(_dft_b_fwd_kernel, scale=scale),
        grid=(r // DFT_KK, c // DFT_CB, p),
        in_specs=[pl.BlockSpec((None, DFT_KK, r, DFT_CB), lambda k, cc, pp: (pp, k, 0, cc)),
                  pl.BlockSpec((2 * r, 2 * r), lambda k, cc, pp: (0, 0))],
        out_specs=pl.BlockSpec((None, 2, DFT_KK, r, DFT_CB), lambda k, cc, pp: (pp, 0, k, 0, cc)),
        out_shape=jax.ShapeDtypeStruct((p, 2, r, r, c), F32),
        compiler_params=_cparams(("parallel", "parallel", "arbitrary")),
        name="dft_stage_b_filter",
    )(a1, fb)


def _dft_c_gate_kernel(g_ref, m_ref, x_ref, v_ref, bias_ref, o_ref):
    cb = g_ref.shape[-1]
    gr, gi = _unpack_complex(g_ref[...].reshape(DFT_RADIX * SUBLANES, cb))
    g = jnp.concatenate([gr, gi], axis=0).astype(BF16)
    y = jnp.dot(m_ref[...], g, preferred_element_type=F32).reshape(o_ref.shape)
    o_ref[...] = x_ref[...] * (y + v_ref[...] * bias_ref[...])


def _dft_c_gate(g5, m_c, xg, xg_off, v, v_off, bias):
    p, r, nbc, _, c = g5.shape
    half = r // 2
    tspec = lambda off: pl.BlockSpec((None, 2, half, None, SUBLANES, DFT_CB),
                                     lambda bc, pp, cc: (pp, 0, 0, bc, 0, cc + off // DFT_CB))
    return pl.pallas_call(
        _dft_c_gate_kernel,
        grid=(nbc, p, c // DFT_CB),
        in_specs=[
            pl.BlockSpec((None, r, None, SUBLANES, DFT_CB), lambda bc, pp, cc: (pp, 0, bc, 0, cc)),
            pl.BlockSpec((None,) + m_c.shape[1:], lambda bc, pp, cc: (bc, 0, 0)),
            tspec(xg_off), tspec(v_off),
            pl.BlockSpec((1, DFT_CB), lambda bc, pp, cc: (0, cc)),
        ],
        out_specs=tspec(0),
        out_shape=jax.ShapeDtypeStruct((p, 2, half, nbc, SUBLANES, c), F32),
        compiler_params=_cparams(("arbitrary", "arbitrary", "arbitrary")),
        name="dft_stage_c_gate",
    )(g5, m_c, xg, v, bias)


def _dft_tables(seq_len):
    r = DFT_RADIX
    n = 2 * seq_len
    nbc = r // SUBLANES
    i32 = jnp.int32
    nt = r * SUBLANES
    idx = jnp.arange(r, dtype=i32)
    alpha = ((idx[:, None] * idx[None, :]) % r).astype(F32) * (2.0 * math.pi / r)
    beta = ((idx[:, None] * idx[None, :]) % n).astype(F32) * (2.0 * math.pi / n)
    ca, sa, cb, sb = jnp.cos(alpha), jnp.sin(alpha), jnp.cos(beta), jnp.sin(beta)

    def by_time(t, pair):
        if pair:
            t = jnp.broadcast_to(t[:, None, None, :r // 2, None], (r, SUBLANES, 2, r // 2, SUBLANES))
        else:
            t = jnp.broadcast_to(t[:, None, :, None], (r, SUBLANES, r, SUBLANES))
        return t.reshape(nt, nt)

    def by_offset(t):
        return jnp.transpose(t.reshape(r, nbc, SUBLANES), (1, 0, 2)).reshape(nbc, nt, 1)

    fi = lax.broadcasted_iota(i32, (nt, nt), 0)
    ti = lax.broadcasted_iota(i32, (nt, nt), 1)
    same_row = (fi % SUBLANES) == (ti % SUBLANES)
    first_seq = ti < nt // 2

    def cos_sin(pair):
        ac, as_, bc_, bs_ = by_time(ca, pair), by_time(sa, pair), by_offset(cb), by_offset(sb)
        c = jnp.where(same_row, ac * bc_ - as_ * bs_, 0.0)
        s = jnp.where(same_row, as_ * bc_ + ac * bs_, 0.0)
        return c, s

    c, s = cos_sin(True)
    re_rows, im_rows = jnp.where(first_seq, c, s), jnp.where(first_seq, -s, c)
    m_pair = jnp.concatenate([re_rows, im_rows], axis=1).astype(BF16)
    m_inv = jnp.concatenate([jnp.swapaxes(re_rows, 1, 2), jnp.swapaxes(im_rows, 1, 2)], axis=2).astype(BF16)
    c, s = cos_sin(False)
    m_real = jnp.concatenate([c, -s], axis=1).astype(BF16)
    kb = idx
    angb = ((kb[:, None] * kb[None, :]) % r).astype(F32) * (2.0 * math.pi / r)
    cb_, sb_ = jnp.cos(angb), jnp.sin(angb)
    fb = jnp.block([[cb_, sb_], [-sb_, cb_]]).astype(BF16)
    fbi = jnp.block([[cb_, -sb_], [sb_, cb_]]).astype(BF16)
    return m_pair, m_real, m_inv, fb, fbi


def _filter_features(seq_len):
    n2 = 2 * seq_len
    n = jnp.arange(n2)
    pos = jnp.where(n <= seq_len, n, n2 - n)
    pos = jnp.minimum(pos, seq_len - 1)
    t_lin = jnp.linspace(0.0, 1.0, seq_len, dtype=F32)
    omega = 2.0 * math.pi * jnp.arange(seq_len, dtype=F32) / seq_len
    bands = jnp.linspace(1e-4, FILTER_BANDS - 1, FILTER_BANDS, dtype=F32)
    phase = omega[:, None] * bands[None, :]
    z = jnp.concatenate([t_lin[:, None], jnp.cos(phase), -jnp.sin(phase)], axis=-1)
    z = jnp.pad(z, ((0, 0), (0, FILTER_HIDDEN - FILTER_EMB)))
    return z[pos]


ATTN_Q_TILES = 2


def _attn_kernel(lq1_ref, lk1_ref, lq2_ref, lk2_ref, subln_ref, qt_ref, k_ref, vt_ref, o_ref,
                 acc_scr, s_scr, *, lam_init):
    qt = jnp.concatenate([qt_ref[t] for t in range(qt_ref.shape[0])], axis=1)
    tq = qt.shape[1]
    tc = vt_ref.shape[2]
    row = lax.broadcasted_iota(jnp.int32, qt.shape, 0)
    zero = jnp.zeros_like(qt)
    qm = (jnp.where(row < HEAD_DIM, qt, zero), jnp.where(row >= HEAD_DIM, qt, zero))
    acc_scr[...] = jnp.zeros(acc_scr.shape, F32)
    n_chunks = vt_ref.shape[0]

    def scores(c, slot):
        k_c = k_ref[pl.ds(pl.multiple_of(c * tc, tc), tc), :]
        cms = []
        for i in range(2):
            st = jnp.dot(k_c, qm[i], preferred_element_type=F32)
            s_scr[slot, i] = st
            cms.append(jnp.max(st, axis=0, keepdims=True))
        return tuple(cms)

    def softmax_pv(c, slot, cms, ms):
        vt_c = vt_ref[c]
        new_m = []
        for i in range(2):
            m_new = jnp.maximum(ms[i], cms[i])
            alpha = jnp.exp2(ms[i] - m_new)
            p = jnp.exp2(s_scr[slot, i] - m_new)
            acc_scr[i] = alpha * acc_scr[i] + jnp.dot(vt_c, p.astype(BF16), preferred_element_type=F32)
            new_m.append(m_new)
        return tuple(new_m)

    def pair(j, carry):
        cm, ms = carry
        c = 2 * j
        cm1 = scores(c + 1, 1)
        ms = softmax_pv(c, 0, cm, ms)
        cm2 = scores(c + 2, 0)
        ms = softmax_pv(c + 1, 1, cm1, ms)
        return cm2, ms

    neg = jnp.full((1, tq), -jnp.inf, F32)
    cm, ms = lax.fori_loop(0, n_chunks // 2 - 1, pair, (scores(0, 0), (neg, neg)), unroll=True)
    cm1 = scores(n_chunks - 1, 1)
    ms = softmax_pv(n_chunks - 2, 0, cm, ms)
    ms = softmax_pv(n_chunks - 1, 1, cm1, ms)

    lam = (jnp.exp(jnp.sum(lq1_ref[...] * lk1_ref[...], axis=1, keepdims=True))
           - jnp.exp(jnp.sum(lq2_ref[...] * lk2_ref[...], axis=1, keepdims=True)) + lam_init)
    vd = V_HEAD_DIM
    ot = (acc_scr[0, :vd] * (1.0 / acc_scr[0, vd:vd + 1])
          - lam * (acc_scr[1, :vd] * (1.0 / acc_scr[1, vd:vd + 1])))
    o = _rms(ot.T, subln_ref[...]) * (1.0 - lam_init)
    o_ref[...] = o.astype(o_ref.dtype)


def _attention(qt, k, vt, lq1, lk1, lq2, lk2, subln, lam_init):
    n_seq, n_heads, n_tiles, _, tc = qt.shape
    seq_len = k.shape[2]
    tq = ATTN_Q_TILES * tc
    v_rows = vt.shape[3]
    vec = lambda a: pl.BlockSpec(a.shape, lambda s, h, i: (0, 0))
    return pl.pallas_call(
        functools.partial(_attn_kernel, lam_init=lam_init),
        grid=(n_seq, n_heads, seq_len // tq),
        in_specs=[
            vec(lq1), vec(lk1), vec(lq2), vec(lk2), vec(subln),
            pl.BlockSpec((None, None, ATTN_Q_TILES, LANES, tc), lambda s, h, i: (s, h, i, 0, 0)),
            pl.BlockSpec((None, None, seq_len, LANES), lambda s, h, i: (s, h, 0, 0)),
            pl.BlockSpec((None, None, n_tiles, v_rows, tc), lambda s, h, i: (s, h, 0, 0, 0)),
        ],
        out_specs=pl.BlockSpec((None, tq, LANES), lambda s, h, i: (s, i, h)),
        out_shape=jax.ShapeDtypeStruct((n_seq, seq_len, n_heads * V_HEAD_DIM), BF16),
        scratch_shapes=[pltpu.VMEM((2, v_rows, tq), F32), pltpu.VMEM((2, 2, tc, tq), F32)],
        compiler_params=_cparams(("parallel", "parallel", "arbitrary")),
        name="diff_attention",
    )(lq1, lk1, lq2, lk2, subln, qt, k, vt)


def _out_proj_kernel(z_ref, a_ref, gh_ref, wt_ref, wb_ref, gp_ref, x_ref, o_ref):
    hn = _rms(z_ref[...], gh_ref[...]).astype(BF16)
    mix = (jnp.dot(hn, wt_ref[...], preferred_element_type=F32)
           + jnp.dot(a_ref[...], wb_ref[...], preferred_element_type=F32))
    o_ref[...] = x_ref[...] + _rms(mix, gp_ref[...])


def _out_proj(z, at, g_hy, w_out, g_post, x, tm=512):
    t_tok = x.shape[0]
    row = lambda w: pl.BlockSpec((tm, w), lambda i: (i, 0))
    return pl.pallas_call(
        _out_proj_kernel,
        grid=(t_tok // tm,),
        in_specs=[
            row(HYENA_WIDTH), row(ATTN_WIDTH),
            pl.BlockSpec((1, HYENA_WIDTH), lambda i: (0, 0)),
            pl.BlockSpec((HYENA_WIDTH, D_MODEL), lambda i: (0, 0)),
            pl.BlockSpec((ATTN_WIDTH, D_MODEL), lambda i: (1, 0)),
            pl.BlockSpec((1, D_MODEL), lambda i: (0, 0)),
            row(D_MODEL),
        ],
        out_specs=row(D_MODEL),
        out_shape=jax.ShapeDtypeStruct(x.shape, F32),
        compiler_params=_cparams(("parallel",)),
        name="out_proj",
    )(z, at, g_hy, w_out, w_out, g_post, x)


FFN_TF = 512


def _ffn_kernel(x_ref, gpre_ref, wg_ref, wu_ref, wd_ref, gpost_ref, o_ref, h_scr, acc_scr):
    f = pl.program_id(1)

    @pl.when(f == 0)
    def _():
        h_scr[...] = _rms(x_ref[...], gpre_ref[...]).astype(BF16)
        acc_scr[...] = jnp.zeros(acc_scr.shape, F32)

    h = h_scr[...]
    gate = jnp.dot(h, wg_ref[...], preferred_element_type=F32)
    up = jnp.dot(h, wu_ref[...], preferred_element_type=F32)
    act = (gate * jax.nn.sigmoid(gate) * up).astype(BF16)
    acc_scr[...] += jnp.dot(act, wd_ref[...], preferred_element_type=F32)

    @pl.when(f == pl.num_programs(1) - 1)
    def _():
        o_ref[...] = x_ref[...] + _rms(acc_scr[...], gpost_ref[...])


def _ffn(x, g_pre, w_gate, w_up, w_down, g_post, tm=512):
    t_tok = x.shape[0]
    return pl.pallas_call(
        _ffn_kernel,
        grid=(t_tok // tm, D_FF // FFN_TF),
        in_specs=[
            pl.BlockSpec((tm, D_MODEL), lambda i, f: (i, 0)),
            pl.BlockSpec((1, D_MODEL), lambda i, f: (0, 0)),
            pl.BlockSpec((None, D_MODEL, FFN_TF), lambda i, f: (f, 0, 0)),
            pl.BlockSpec((None, D_MODEL, FFN_TF), lambda i, f: (f, 0, 0)),
            pl.BlockSpec((FFN_TF, D_MODEL), lambda i, f: (f, 0)),
            pl.BlockSpec((1, D_MODEL), lambda i, f: (0, 0)),
        ],
        out_specs=pl.BlockSpec((tm, D_MODEL), lambda i, f: (i, 0)),
        out_shape=jax.ShapeDtypeStruct(x.shape, F32),
        scratch_shapes=[pltpu.VMEM((tm, D_MODEL), BF16), pltpu.VMEM((tm, D_MODEL), F32)],
        compiler_params=_cparams(("parallel", "arbitrary")),
        name="ffn",
    )(x, g_pre, w_gate, w_up, w_down, g_post)


def _rope_tables(seq_len):
    inv_freq = 1.0 / (ROPE_THETA ** (jnp.arange(0, HEAD_DIM, 2, dtype=F32) / HEAD_DIM))
    ang = jnp.arange(seq_len, dtype=F32)[:, None] * inv_freq[None, :]
    cos, sin = jnp.cos(ang), jnp.sin(ang)
    zero = jnp.zeros_like(sin)
    reps = LANES // HEAD_DIM
    cos_t = jnp.tile(jnp.concatenate([cos, cos], axis=1), (1, reps))
    sa_t = jnp.tile(jnp.concatenate([-sin, zero], axis=1), (1, reps))
    sb_t = jnp.tile(jnp.concatenate([zero, sin], axis=1), (1, reps))
    return cos_t, sa_t, sb_t


def _hyena_conv(u_v, v_off, xg, xg_off, kf, order, bias, tabs):
    m_pair, _, m_inv, fb, fbi = tabs
    a1 = _dft_a(u_v, m_pair, v_off)
    p, r, nbc, sl, c = a1.shape
    g = _dft_b_conv(a1.reshape(p, r, nbc * sl, c), kf, order, fb, fbi)
    return _dft_c_gate(g.reshape(a1.shape), m_inv, xg, xg_off, u_v, v_off, bias)


def _filter_spectra(li, seq_len, tabs, zfeat, delta, w):
    (filt_w1, filt_b1, filt_f1, filt_w2, filt_b2, filt_f2, filt_w3, filt_b3, filt_f3, filt_w4) = w[7:17]
    row = lambda a: a[li].reshape(1, -1)
    r = DFT_RADIX
    nbc = r // SUBLANES
    pad_w1 = jnp.pad(filt_w1[li], ((0, FILTER_HIDDEN - FILTER_EMB), (0, 0)))
    kt = _filters(zfeat, pad_w1, row(filt_b1), row(filt_f1), filt_w2[li], row(filt_b2), row(filt_f2),
                  filt_w3[li], row(filt_b3), row(filt_f3), filt_w4[li].astype(BF16), delta, seq_len)
    ka = _dft_a(kt.reshape(2, r, nbc, SUBLANES, HYENA_WIDTH), tabs[1], 0)
    return _dft_b_fwd(ka.reshape(2, r, r, HYENA_WIDTH), tabs[3], 1.0 / (2 * seq_len))


def _layer(x, li, n_seq, seq_len, rope, tabs, kf, w):
    (norm_mix_pre, norm_mix_post, norm_ffn_pre, norm_ffn_post, w_in, conv_w, conv_b,
     _, _, _, _, _, _, _, _, _, _,
     hyena_bias, hyena_norm, lam_q1, lam_k1, lam_q2, lam_k2, subln,
     w_out, w_gate, w_up, w_down) = w
    lam_init = 0.8 - 0.6 * math.exp(-0.3 * li)
    row = lambda a: a[li].reshape(1, -1)
    r = DFT_RADIX
    nbc = r // SUBLANES
    n_pair = n_seq // 2

    hy, q, k, v = _in_proj(x, row(norm_mix_pre), w_in[li], *rope, n_seq, seq_len)

    cw = jnp.pad(conv_w[li], ((0, SUBLANES - conv_w.shape[1]), (0, 0)))
    u = _short_conv(hy.reshape(n_seq, seq_len, HYENA_IN), cw, row(conv_b))
    u6 = u.reshape(n_pair, 2, r // 2, nbc, SUBLANES, HYENA_IN)
    z1 = _hyena_conv(u6, 2 * HYENA_WIDTH, u6, 0, kf, 0, hyena_bias[li, 0].reshape(1, -1), tabs)
    z2 = _hyena_conv(z1, 0, u6, HYENA_WIDTH, kf, 1, hyena_bias[li, 1].reshape(1, -1), tabs)

    at = _attention(q, k, v, row(lam_q1), row(lam_k1), row(lam_q2), row(lam_k2), row(subln), lam_init)

    x = _out_proj(z2.reshape(n_seq * seq_len, HYENA_WIDTH), at.reshape(n_seq * seq_len, ATTN_WIDTH),
                  row(hyena_norm), w_out[li], row(norm_mix_post), x)
    return _ffn(x, row(norm_ffn_pre), w_gate[li], w_up[li], w_down[li], row(norm_ffn_post))


def kernel(x_prompt, x_sample, norm_mix_pre, norm_mix_post, norm_ffn_pre, norm_ffn_post, w_in, conv_w, conv_b, filt_w1, filt_b1, filt_f1, filt_w2, filt_b2, filt_f2, filt_w3, filt_b3, filt_f3, filt_w4, hyena_bias, hyena_norm, lam_q1, lam_k1, lam_q2, lam_k2, subln, w_out, w_gate, w_up, w_down):
    assert x_prompt.shape[1:] == x_sample.shape[1:]
    n_p, seq_len, d = x_prompt.shape
    n_s = x_sample.shape[0]
    assert n_p % 2 == 0 and n_s % 2 == 0 and seq_len == DFT_RADIX * DFT_RADIX // 2 and d == D_MODEL
    xs = [x_prompt.reshape(n_p * seq_len, d), x_sample.reshape(n_s * seq_len, d)]
    def col_tiles(wt, tn):
        dep, kk, nn = wt.shape
        return jnp.transpose(wt.astype(BF16).reshape(dep, kk, nn // tn, tn), (0, 2, 1, 3))

    w = (norm_mix_pre, norm_mix_post, norm_ffn_pre, norm_ffn_post, col_tiles(w_in, IN_TN), conv_w, conv_b,
         filt_w1, filt_b1, filt_f1, filt_w2, filt_b2, filt_f2, filt_w3, filt_b3, filt_f3, filt_w4,
         hyena_bias, hyena_norm, lam_q1, lam_k1, lam_q2, lam_k2, subln,
         w_out.astype(BF16), col_tiles(w_gate, FFN_TF), col_tiles(w_up, FFN_TF), w_down.astype(BF16))
    rope = _rope_tables(seq_len)
    tabs = _dft_tables(seq_len)
    zfeat = _filter_features(seq_len)
    delta = jnp.abs(jnp.linspace(MIN_DECAY, MAX_DECAY, HYENA_WIDTH, dtype=F32)).reshape(1, -1)
    for li in range(DEPTH):
        kf = _filter_spectra(li, seq_len, tabs, zfeat, delta, w)
        xs = [_layer(x, li, x.shape[0] // seq_len, seq_len, rope, tabs, kf, w) for x in xs]
    return (xs[0].reshape(n_p, seq_len, d), xs[1].reshape(n_s, seq_len, d))
```
